```python
import jax
import jax.numpy as jnp
from jax import lax
import numpy as np

D_MODEL = 4096
BATCH = 1
SEQ = 16384
DEPTH = 4

N_MIXERS = 3
RMS_EPS = 1e-6
CONV_WIDTH = 31
DIL_HEAD_DIM = 128
DIL_HEADS = D_MODEL // (2 * DIL_HEAD_DIM)
DIL_PATTERNS = ((128, 1), (512, 4), (2048, 16))
N_DIL_GROUPS = len(DIL_PATTERNS)
ROT_DIM = DIL_HEAD_DIM // 4
ROPE_THETA = 500000.0
ATTN_QKV_COLS = N_DIL_GROUPS * 3 * DIL_HEADS * DIL_HEAD_DIM
ATTN_OUT_DIM = DIL_HEADS * DIL_HEAD_DIM
MLSTM_HEADS = 8
MLSTM_DV = D_MODEL // MLSTM_HEADS
MLSTM_DQK = MLSTM_DV // 2
MLSTM_CHUNK = 128
GATE_SOFTCAP = 15.0
FORGET_BIAS = 3.0
MLSTM_IN_COLS = 2 * MLSTM_HEADS * MLSTM_DQK + 2 * D_MODEL + 4 * MLSTM_HEADS
N_EXPERTS = 16
EXPERT_FF = 3 * D_MODEL // 16
CAPACITY_FACTOR = 2
N_CONV_LAYERS = (DEPTH + 2) // 3
N_ATTN_LAYERS = (DEPTH + 1) // 3
N_MLSTM_LAYERS = DEPTH // 3

kernel_name = 'hybrid_conv_dilated_mlstm_ec_moe_encoder'


def rms_norm(x, g):
    xf = x.astype(jnp.float32)
    y = xf * lax.rsqrt(jnp.mean(xf * xf, axis=-1, keepdims=True) + RMS_EPS)
    return (y * g.astype(jnp.float32)).astype(x.dtype)


def layer_norm(x, g, b):
    xf = x.astype(jnp.float32)
    mu = jnp.mean(xf, axis=-1, keepdims=True)
    xc = xf - mu
    y = xc * lax.rsqrt(jnp.mean(xc * xc, axis=-1, keepdims=True) + RMS_EPS)
    return (y * g.astype(jnp.float32) + b.astype(jnp.float32)).astype(x.dtype)


def soft_cap(t):
    return GATE_SOFTCAP * jnp.tanh(t / GATE_SOFTCAP)


def conformer_conv(h, w_in, b_in, w_dw, b_dw, ln_g, ln_b, w_out, b_out):
    d = h.shape[-1]
    u = h @ w_in + b_in
    a, g = jnp.split(u, 2, axis=-1)
    u = a * jax.nn.sigmoid(g)
    u = lax.conv_general_dilated(
        u, w_dw[:, None, :].astype(u.dtype), window_strides=(1,),
        padding=[(CONV_WIDTH // 2, CONV_WIDTH // 2)],
        dimension_numbers=('NWC', 'WIO', 'NWC'), feature_group_count=d) + b_dw
    u = jax.nn.silu(layer_norm(u, ln_g, ln_b))
    return u @ w_out + b_out


def partial_rotary(t, positions):
    half = ROT_DIM // 2
    inv_freq = ROPE_THETA ** (-jnp.arange(half, dtype=jnp.float32) / half)
    ang = positions.astype(jnp.float32)[..., None] * inv_freq
    ang = ang.reshape(ang.shape[:2] + (1,) * (t.ndim - 3) + (half,))
    cos, sin = jnp.cos(ang), jnp.sin(ang)
    t1 = t[..., :half].astype(jnp.float32)
    t2 = t[..., half:ROT_DIM].astype(jnp.float32)
    rot = jnp.concatenate([t1 * cos - t2 * sin, t2 * cos + t1 * sin], axis=-1).astype(t.dtype)
    return jnp.concatenate([rot, t[..., ROT_DIM:]], axis=-1)


def banded_attention(q, k, v, half):
    L, hd = q.shape[-2], q.shape[-1]
    lead = q.shape[:-2]
    nb = -(-L // half)
    pad = nb * half - L
    no_pad = [(0, 0)] * len(lead)
    qb = jnp.pad(q, no_pad + [(0, pad), (0, 0)]).reshape(lead + (nb, half, hd))

    def key_blocks(t):
        tp = jnp.pad(t, no_pad + [(half, pad + half), (0, 0)]).reshape(lead + (nb + 2, half, hd))
        return jnp.concatenate([tp[..., :-2, :, :], tp[..., 1:-1, :, :], tp[..., 2:, :, :]], axis=-2)

    kb, vb = key_blocks(k), key_blocks(v)
    a = jnp.arange(half)[:, None]
    c = jnp.arange(3 * half)[None, :]
    j = (jnp.arange(nb)[:, None, None] - 1) * half + c
    mask = (c >= a) & (c <= a + 2 * half) & (j >= 0) & (j < L)
    logits = jnp.einsum('...nqd,...nkd->...nqk', qb, kb).astype(jnp.float32)
    logits = jnp.where(mask, logits, -jnp.inf)
    mx = jnp.max(logits, axis=-1, keepdims=True)
    lse = mx + jnp.log(jnp.sum(jnp.exp(logits - mx), axis=-1, keepdims=True))
    p = jnp.exp(logits - lse)
    o = jnp.einsum('...nqk,...nkd->...nqd', p.astype(v.dtype), vb)
    o = o.reshape(lead + (nb * half, hd))[..., :L, :]
    lse = lse.reshape(lead + (nb * half,))[..., :L]
    return o, lse


def to_strided(t, dil):
    b, s, h, hd = t.shape
    return t.reshape(b, s // dil, dil, h, hd).transpose(0, 2, 3, 1, 4)


def dilated_attention(h, positions, w_qkv, w_out):
    B, S, _ = h.shape
    qkv = (h @ w_qkv).reshape(B, S, N_DIL_GROUPS, 3, DIL_HEADS, DIL_HEAD_DIM)
    q = partial_rotary(qkv[:, :, :, 0], positions) * (DIL_HEAD_DIM ** -0.5)
    k = partial_rotary(qkv[:, :, :, 1], positions)
    v = qkv[:, :, :, 2]
    outs, lses = [], []
    for g, (window, dil) in enumerate(DIL_PATTERNS):
        half = window // (2 * dil)
        o, lse = banded_attention(to_strided(q[:, :, g], dil), to_strided(k[:, :, g], dil),
                                  to_strided(v[:, :, g], dil), half)
        outs.append(o.transpose(0, 3, 1, 2, 4).reshape(B, S, DIL_HEADS, DIL_HEAD_DIM))
        lses.append(lse.transpose(0, 3, 1, 2).reshape(B, S, DIL_HEADS))
    alpha = jax.nn.softmax(jnp.stack(lses), axis=0)
    o = jnp.sum(alpha[..., None] * jnp.stack(outs).astype(jnp.float32), axis=0).astype(h.dtype)
    return o.reshape(B, S, ATTN_OUT_DIM) @ w_out


def mlstm_direction(q, k, v, log_i, log_f):
    B, H, S, dk = q.shape
    dv = v.shape[-1]
    L = MLSTM_CHUNK
    n_chunks = S // L

    def chunk(t):
        return jnp.moveaxis(t.reshape(t.shape[:2] + (n_chunks, L) + t.shape[3:]), 2, 0)

    causal = jnp.tril(jnp.ones((L, L), dtype=bool))

    def step(carry, inp):
        C, n, m = carry
        qc, kc, vc, li, lf = inp
        b = jnp.cumsum(lf, axis=-1)
        d_intra = jnp.where(causal, b[..., :, None] - b[..., None, :] + li[..., None, :], -jnp.inf)
        d_inter = b + m[..., None]
        m_t = jnp.maximum(d_inter, jnp.max(d_intra, axis=-1))
        w_inter = jnp.exp(d_inter - m_t)
        s = jnp.einsum('bhtd,bhsd->bhts', qc, kc) * jnp.exp(d_intra - m_t[..., None])
        num = (w_inter[..., None] * jnp.einsum('bhtd,bhde->bhte', qc, C)
               + jnp.einsum('bhts,bhse->bhte', s, vc))
        den = w_inter * jnp.einsum('bhtd,bhd->bht', qc, n) + jnp.sum(s, axis=-1)
        h = num / jnp.maximum(jnp.abs(den), jnp.exp(-m_t))[..., None]
        b_last = b[..., -1]
        a = b_last[..., None] - b + li
        m_new = jnp.maximum(b_last + m, jnp.max(a, axis=-1))
        decay = jnp.exp(b_last + m - m_new)
        wk = jnp.exp(a - m_new[..., None])[..., None] * kc
        C_new = decay[..., None, None] * C + jnp.einsum('bhsd,bhse->bhde', wk, vc)
        n_new = decay[..., None] * n + jnp.sum(wk, axis=2)
        return (C_new, n_new, m_new), h

    init = (jnp.zeros((B, H, dk, dv), jnp.float32), jnp.zeros((B, H, dk), jnp.float32),
            jnp.zeros((B, H), jnp.float32))
    _, h = lax.scan(step, init, (chunk(q), chunk(k), chunk(v), chunk(log_i), chunk(log_f)))
    return jnp.moveaxis(h, 0, 2).reshape(B, H, S, dv)


def mlstm_mixer(h, w_in, b_gates, head_norm, w_out):
    B, S, D = h.shape
    H = MLSTM_HEADS
    nqk = H * MLSTM_DQK
    proj = h @ w_in
    q = proj[..., :nqk]
    k = proj[..., nqk:2 * nqk]
    v = proj[..., 2 * nqk:2 * nqk + D]
    o = proj[..., 2 * nqk + D:2 * nqk + 2 * D]
    gates = proj[..., 2 * nqk + 2 * D:] + b_gates

    def heads(t, d):
        return t.reshape(B, S, H, d).transpose(0, 2, 1, 3).astype(jnp.float32)

    q = heads(q, MLSTM_DQK) * (MLSTM_DQK ** -0.5)
    k = heads(k, MLSTM_DQK)
    v = heads(v, MLSTM_DV)
    gates = soft_cap(gates.astype(jnp.float32)).reshape(B, S, 4, H).transpose(2, 0, 3, 1)
    i_f, f_f, i_b, f_b = gates[0], gates[1], gates[2], gates[3]
    h_fwd = mlstm_direction(q, k, v, i_f, jax.nn.log_sigmoid(f_f))

    def flip(t):
        return jnp.flip(t, axis=2)

    h_bwd = flip(mlstm_direction(flip(q), flip(k), flip(v), flip(i_b), jax.nn.log_sigmoid(flip(f_b))))
    hs = (h_fwd + h_bwd).transpose(0, 2, 1, 3)
    hs = hs * lax.rsqrt(jnp.mean(hs * hs, axis=-1, keepdims=True) + RMS_EPS)
    hs = hs * head_norm.astype(jnp.float32).reshape(H, MLSTM_DV)
    y = (hs.reshape(B, S, H * MLSTM_DV) * jax.nn.sigmoid(o.astype(jnp.float32))).astype(h.dtype)
    return y @ w_out


def expert_choice_ffn(h, w_router, w_gate, w_up, w_down):
    B, S, D = h.shape
    E = w_router.shape[-1]
    cap = CAPACITY_FACTOR * S // E
    logits = h.astype(jnp.float32) @ w_router.astype(jnp.float32)
    affinity = jax.nn.softmax(logits, axis=-1)
    gate, idx = lax.top_k(affinity.transpose(0, 2, 1), cap)
    flat = idx.reshape(B, E * cap)
    xg = jax.vmap(lambda hb, ib: hb[ib])(h, flat).reshape(B, E, cap, D)
    g = jnp.einsum('becd,edf->becf', xg, w_gate)
    u = jnp.einsum('becd,edf->becf', xg, w_up)
    y = jnp.einsum('becf,efd->becd', jax.nn.silu(g) * u, w_down) * gate[..., None].astype(h.dtype)
    return jax.vmap(lambda ib, yb: jnp.zeros((S, D), h.dtype).at[ib].add(yb))(
        flat, y.reshape(B, E * cap, D))


def setup_inputs(seed: int = 0) -> dict:
    key = jax.random.key(seed)
    ks = iter(jax.random.split(key, 32))
    D, E, F = D_MODEL, N_EXPERTS, EXPERT_FF

    def dense(shape, fan_in):
        return jax.random.normal(next(ks), shape, jnp.float32) * (fan_in ** -0.5)

    def gain(shape):
        return 1.0 + 0.02 * jax.random.normal(next(ks), shape, jnp.float32)

    def bias(shape):
        return 0.02 * jax.random.normal(next(ks), shape, jnp.float32)

    x = jax.random.normal(next(ks), (BATCH, SEQ, D), jnp.float32)
    positions = (jax.random.randint(next(ks), (BATCH, 1), 0, 4096, dtype=jnp.int32)
                 + jnp.arange(SEQ, dtype=jnp.int32)[None, :])
    gate_base = jnp.repeat(jnp.array([0.0, FORGET_BIAS, 0.0, FORGET_BIAS], jnp.float32), MLSTM_HEADS)
    mlstm_b_gates = gate_base + 0.1 * jax.random.normal(next(ks), (N_MLSTM_LAYERS, 4 * MLSTM_HEADS), jnp.float32)
    return {
        'x': x,
        'positions': positions,
        'mix_norm': gain((DEPTH, D)),
        'ffn_norm': gain((DEPTH, D)),
        'conv_w_in': dense((N_CONV_LAYERS, D, 2 * D), D),
        'conv_b_in': bias((N_CONV_LAYERS, 2 * D)),
        'conv_w_dw': dense((N_CONV_LAYERS, CONV_WIDTH, D), CONV_WIDTH),
        'conv_b_dw': bias((N_CONV_LAYERS, D)),
        'conv_ln_g': gain((N_CONV_LAYERS, D)),
        'conv_ln_b': bias((N_CONV_LAYERS, D)),
        'conv_w_out': dense((N_CONV_LAYERS, D, D), D),
        'conv_b_out': bias((N_CONV_LAYERS, D)),
        'attn_w_qkv': dense((N_ATTN_LAYERS, D, ATTN_QKV_COLS), D),
        'attn_w_out': dense((N_ATTN_LAYERS, ATTN_OUT_DIM, D), ATTN_OUT_DIM),
        'mlstm_w_in': dense((N_MLSTM_LAYERS, D, MLSTM_IN_COLS), D),
        'mlstm_b_gates': mlstm_b_gates,
        'mlstm_head_norm': gain((N_MLSTM_LAYERS, MLSTM_HEADS * MLSTM_DV)),
        'mlstm_w_out': dense((N_MLSTM_LAYERS, MLSTM_HEADS * MLSTM_DV, D), MLSTM_HEADS * MLSTM_DV),
        'router_w': dense((DEPTH, D, E), D),
        'moe_w_gate': dense((DEPTH, E, D, F), D),
        'moe_w_up': dense((DEPTH, E, D, F), D),
        'moe_w_down': dense((DEPTH, E, F, D), F),
        'final_norm': gain((D,)),
    }


def reference(x, positions, mix_norm, ffn_norm, conv_w_in, conv_b_in, conv_w_dw, conv_b_dw,
              conv_ln_g, conv_ln_b, conv_w_out, conv_b_out, attn_w_qkv, attn_w_out,
              mlstm_w_in, mlstm_b_gates, mlstm_head_norm, mlstm_w_out,
              router_w, moe_w_gate, moe_w_up, moe_w_down, final_norm):
    for i in range(DEPTH):
        kind, j = i % N_MIXERS, i // N_MIXERS
        h = rms_norm(x, mix_norm[i])
        if kind == 0:
            mix = conformer_conv(h, conv_w_in[j], conv_b_in[j], conv_w_dw[j], conv_b_dw[j],
                                 conv_ln_g[j], conv_ln_b[j], conv_w_out[j], conv_b_out[j])
        elif kind == 1:
            mix = dilated_attention(h, positions, attn_w_qkv[j], attn_w_out[j])
        else:
            mix = mlstm_mixer(h, mlstm_w_in[j], mlstm_b_gates[j], mlstm_head_norm[j], mlstm_w_out[j])
        x = x + mix
        x = x + expert_choice_ffn(rms_norm(x, ffn_norm[i]), router_w[i], moe_w_gate[i],
                                  moe_w_up[i], moe_w_down[i])
    return rms_norm(x, final_norm)
```

```python
import functools

import jax
import jax.numpy as jnp
from jax import lax
from jax.experimental import pallas as pl
from jax.experimental.pallas import tpu as pltpu

F32 = jnp.float32
BF16 = jnp.bfloat16
I32 = jnp.int32

RMS_EPS = 1e-6
CONV_WIDTH = 31
CONV_HALO = 16
HEAD_DIM = 128
ATTN_HEADS = 16
DIL_PATTERNS = ((128, 1), (512, 4), (2048, 16))
ATTN_HALF = 64
ROT_DIM = 32
ROPE_THETA = 500000.0
GROUP_COLS = 3 * ATTN_HEADS * HEAD_DIM
HEADS_COLS = ATTN_HEADS * HEAD_DIM
MLSTM_HEADS = 8
MLSTM_DQK = 256
MLSTM_DV = 512
MLSTM_CHUNK = 128
GATE_SOFTCAP = 15.0
N_EXPERTS = 16
CAPACITY_FACTOR = 2
LANES = 128
VMEM_LIMIT = 56 * 1024 * 1024


def _cp(sem, vmem=VMEM_LIMIT):
    return pltpu.CompilerParams(dimension_semantics=sem, vmem_limit_bytes=vmem)


def _sds(shape, dtype):
    return jax.ShapeDtypeStruct(shape, dtype)


def _rmsnorm_body(x_ref, g_ref, o_ref):
    x = x_ref[...]
    ms = jnp.mean(x * x, axis=-1, keepdims=True)
    o_ref[...] = (x * lax.rsqrt(ms + RMS_EPS) * g_ref[...]).astype(o_ref.dtype)


def rmsnorm(x, g, out_dtype, tm=256):
    S, D = x.shape
    return pl.pallas_call(
        _rmsnorm_body,
        grid=(S // tm,),
        in_specs=[pl.BlockSpec((tm, D), lambda i: (i, 0)),
                  pl.BlockSpec((1, D), lambda i: (0, 0))],
        out_specs=pl.BlockSpec((tm, D), lambda i: (i, 0)),
        out_shape=_sds((S, D), out_dtype),
        compiler_params=_cp(("parallel",)),
        name="rmsnorm",
    )(x, g.reshape(1, D))


def _rmsnorm_proj_body(x_ref, g_ref, wt_ref, b_ref, o_ref, p_ref):
    x = x_ref[...]
    ms = jnp.mean(x * x, axis=-1, keepdims=True)
    y = x * lax.rsqrt(ms + RMS_EPS) * g_ref[...]
    o_ref[...] = y.astype(o_ref.dtype)
    p = lax.dot_general(wt_ref[...], y, (((1,), (1,)), ((), ())),
                        precision=lax.Precision.HIGHEST, preferred_element_type=F32)
    p_ref[...] = p + b_ref[...]


def rmsnorm_proj(x, g, w, b, out_dtype, tm=256):
    S, D = x.shape
    N = w.shape[1]
    return pl.pallas_call(
        _rmsnorm_proj_body,
        grid=(S // tm,),
        in_specs=[pl.BlockSpec((tm, D), lambda i: (i, 0)),
                  pl.BlockSpec((1, D), lambda i: (0, 0)),
                  pl.BlockSpec((N, D), lambda i: (0, 0)),
                  pl.BlockSpec((N, 1), lambda i: (0, 0))],
        out_specs=[pl.BlockSpec((tm, D), lambda i: (i, 0)),
                   pl.BlockSpec((N, tm), lambda i: (0, i))],
        out_shape=[_sds((S, D), out_dtype), _sds((N, S), F32)],
        compiler_params=_cp(("parallel",)),
        name="rmsnorm_proj",
    )(x, g.reshape(1, D), w.T, b.reshape(N, 1))


def _dot(a, b):
    return jnp.dot(a, b, preferred_element_type=F32)


def _mm_plain_body(a_ref, b_ref, o_ref):
    o_ref[...] = _dot(a_ref[...], b_ref[...]).astype(o_ref.dtype)


def matmul_plain(a, b, out_dtype, tm=512, tn=1024):
    M, K = a.shape
    N = b.shape[1]
    return pl.pallas_call(
        _mm_plain_body,
        grid=(N // tn, M // tm),
        in_specs=[pl.BlockSpec((tm, K), lambda j, i: (i, 0)),
                  pl.BlockSpec((K, tn), lambda j, i: (0, j))],
        out_specs=pl.BlockSpec((tm, tn), lambda j, i: (i, j)),
        out_shape=_sds((M, N), out_dtype),
        compiler_params=_cp(("parallel", "parallel")),
        name="matmul_plain",
    )(a, b)


def _mm_res_body(a_ref, b_ref, bias_ref, r_ref, o_ref):
    o_ref[...] = _dot(a_ref[...], b_ref[...]) + bias_ref[...] + r_ref[...]


def matmul_bias_residual(a, b, bias, res, tm=512, tn=1024):
    M, K = a.shape
    N = b.shape[1]
    return pl.pallas_call(
        _mm_res_body,
        grid=(N // tn, M // tm),
        in_specs=[pl.BlockSpec((tm, K), lambda j, i: (i, 0)),
                  pl.BlockSpec((K, tn), lambda j, i: (0, j)),
                  pl.BlockSpec((1, tn), lambda j, i: (0, j)),
                  pl.BlockSpec((tm, tn), lambda j, i: (i, j))],
        out_specs=pl.BlockSpec((tm, tn), lambda j, i: (i, j)),
        out_shape=_sds((M, N), F32),
        compiler_params=_cp(("parallel", "parallel")),
        name="matmul_bias_residual",
    )(a, b, bias.reshape(1, N), res)


def _mm_glu_body(a_ref, b1_ref, b2_ref, bias1_ref, bias2_ref, o_ref):
    a = a_ref[...]
    u1 = _dot(a, b1_ref[...]) + bias1_ref[...]
    u2 = _dot(a, b2_ref[...]) + bias2_ref[...]
    o_ref[...] = u1 * jax.nn.sigmoid(u2)


def matmul_glu(a, b, bias, tm=512, tn=512):
    M, K = a.shape
    N2 = b.shape[1] // 2
    nb = N2 // tn
    bias = bias.reshape(1, 2 * N2)
    return pl.pallas_call(
        _mm_glu_body,
        grid=(nb, M // tm),
        in_specs=[pl.BlockSpec((tm, K), lambda j, i: (i, 0)),
                  pl.BlockSpec((K, tn), lambda j, i: (0, j)),
                  pl.BlockSpec((K, tn), lambda j, i: (0, j + nb)),
                  pl.BlockSpec((1, tn), lambda j, i: (0, j)),
                  pl.BlockSpec((1, tn), lambda j, i: (0, j + nb))],
        out_specs=pl.BlockSpec((tm, tn), lambda j, i: (i, j)),
        out_shape=_sds((M, N2), F32),
        compiler_params=_cp(("parallel", "parallel")),
        name="matmul_glu",
    )(a, b, b, bias, bias)


def _mm_rope_body(a_ref, b_ref, c_ref, s1_ref, s2_ref, o_ref, *, tn, tiles_per_kind):
    kind = (pl.program_id(0) // tiles_per_kind) % 3
    acc = _dot(a_ref[...], b_ref[...])

    @pl.when(kind == 2)
    def _():
        o_ref[...] = acc.astype(o_ref.dtype)

    @pl.when(kind < 2)
    def _():
        scale = jnp.where(kind == 0, HEAD_DIM ** -0.5, 1.0).astype(F32)
        c, s1, s2 = c_ref[...], s1_ref[...], s2_ref[...]
        for hb in range(tn // HEAD_DIM):
            t = acc[:, hb * HEAD_DIM:(hb + 1) * HEAD_DIM]
            r = (t * c + pltpu.roll(t, ROT_DIM // 2, 1) * s1
                 + pltpu.roll(t, HEAD_DIM - ROT_DIM // 2, 1) * s2)
            o_ref[:, hb * HEAD_DIM:(hb + 1) * HEAD_DIM] = (r * scale).astype(o_ref.dtype)


def matmul_qkv_rope(a, b, tabs, tm=512, tn=1024):
    M, K = a.shape
    N = b.shape[1]
    c, s1, s2 = tabs
    tab_spec = pl.BlockSpec((tm, HEAD_DIM), lambda j, i: (i, 0))
    return pl.pallas_call(
        functools.partial(_mm_rope_body, tn=tn, tiles_per_kind=HEADS_COLS // tn),
        grid=(N // tn, M // tm),
        in_specs=[pl.BlockSpec((tm, K), lambda j, i: (i, 0)),
                  pl.BlockSpec((K, tn), lambda j, i: (0, j)),
                  tab_spec, tab_spec, tab_spec],
        out_specs=pl.BlockSpec((tm, tn), lambda j, i: (i, j)),
        out_shape=_sds((M, N), BF16),
        compiler_params=_cp(("parallel", "parallel")),
        name="matmul_qkv_rope",
    )(a, b, c, s1, s2)


def _rope_tab_body(pos_ref, invf_ref, c_ref, s1_ref, s2_ref):
    half = ROT_DIM // 2
    ang = pos_ref[...].astype(F32) * invf_ref[...]
    cos, sin = jnp.cos(ang), jnp.sin(ang)
    lane = lax.broadcasted_iota(I32, ang.shape, 1)
    c_ref[...] = jnp.where(lane < ROT_DIM, cos, 1.0)
    s1_ref[...] = jnp.where((lane >= half) & (lane < ROT_DIM), sin, 0.0)
    s2_ref[...] = jnp.where(lane < half, -sin, 0.0)


def rope_tables(positions, tm=512):
    S = positions.shape[0]
    half = ROT_DIM // 2
    inv_freq = ROPE_THETA ** (-jnp.arange(half, dtype=F32) / half)
    invf = jnp.zeros((1, HEAD_DIM), F32).at[0, :ROT_DIM].set(jnp.tile(inv_freq, 2))
    spec = pl.BlockSpec((tm, HEAD_DIM), lambda i: (i, 0))
    return pl.pallas_call(
        _rope_tab_body,
        grid=(S // tm,),
        in_specs=[pl.BlockSpec((tm, 1), lambda i: (i, 0)),
                  pl.BlockSpec((1, HEAD_DIM), lambda i: (0, 0))],
        out_specs=[spec, spec, spec],
        out_shape=[_sds((S, HEAD_DIM), F32)] * 3,
        compiler_params=_cp(("parallel",)),
        name="rope_tables",
    )(positions.reshape(S, 1), invf)


def _dwconv_ln_body(up_ref, um_ref, un_ref, w_ref, b_ref, g_ref, beta_ref, o_ref, pad, cv,
                    *, ts, rows, lanes):
    i = pl.program_id(0)
    last = pl.num_programs(0) - 1
    H = CONV_HALO
    D = um_ref.shape[1]
    pad[0:H, :] = jnp.where(i > 0, up_ref[...], 0.0)
    pad[H:H + ts, :] = um_ref[...]
    pad[H + ts:H + ts + H, :] = jnp.where(i < last, un_ref[...], 0.0)
    off = H - CONV_WIDTH // 2
    for r0 in range(0, ts, rows):
        for l0 in range(0, D, lanes):
            acc = jnp.zeros((rows, lanes), F32)
            for k in range(CONV_WIDTH):
                acc = acc + pad[r0 + off + k:r0 + off + k + rows, l0:l0 + lanes] * w_ref[k:k + 1, l0:l0 + lanes]
            cv[r0:r0 + rows, l0:l0 + lanes] = acc
    u = cv[...] + b_ref[...]
    mu = jnp.mean(u, axis=-1, keepdims=True)
    uc = u - mu
    y = uc * lax.rsqrt(jnp.mean(uc * uc, axis=-1, keepdims=True) + RMS_EPS)
    y = y * g_ref[...] + beta_ref[...]
    o_ref[...] = (y * jax.nn.sigmoid(y)).astype(o_ref.dtype)


def dwconv_ln_silu(u, w, b, g, beta, ts=128):
    S, D = u.shape
    H = CONV_HALO
    nh = ts // H
    vec = pl.BlockSpec((1, D), lambda i: (0, 0))
    return pl.pallas_call(
        functools.partial(_dwconv_ln_body, ts=ts, rows=32, lanes=512),
        grid=(S // ts,),
        in_specs=[pl.BlockSpec((H, D), lambda i: (jnp.maximum(i * nh - 1, 0), 0)),
                  pl.BlockSpec((ts, D), lambda i: (i, 0)),
                  pl.BlockSpec((H, D), lambda i: (jnp.minimum((i + 1) * nh, S // H - 1), 0)),
                  pl.BlockSpec((CONV_WIDTH, D), lambda i: (0, 0)),
                  vec, vec, vec],
        out_specs=pl.BlockSpec((ts, D), lambda i: (i, 0)),
        out_shape=_sds((S, D), BF16),
        scratch_shapes=[pltpu.VMEM((ts + 2 * H, D), F32), pltpu.VMEM((ts, D), F32)],
        compiler_params=_cp(("parallel",)),
        name="dwconv_ln_silu",
    )(u, u, u, w, b.reshape(1, D), g.reshape(1, D), beta.reshape(1, D))


ATTN_SUB = 128


def _band_attn_body(q_ref, kp_ref, km_ref, kn_ref, vp_ref, vm_ref, vn_ref, o_ref, lse_ref, kc, vc,
                    *, tq, L):
    i = pl.program_id(1)
    Hh = ATTN_HALF
    kc[0:Hh, :] = kp_ref[...]
    kc[Hh:Hh + tq, :] = km_ref[...]
    kc[Hh + tq:Hh + tq + Hh, :] = kn_ref[...]
    vc[0:Hh, :] = vp_ref[...]
    vc[Hh:Hh + tq, :] = vm_ref[...]
    vc[Hh + tq:Hh + tq + Hh, :] = vn_ref[...]
    nk = ATTN_SUB + 2 * Hh
    t = lax.broadcasted_iota(I32, (ATTN_SUB, nk), 0)
    c = lax.broadcasted_iota(I32, (ATTN_SUB, nk), 1)
    lane_h = lax.broadcasted_iota(I32, (ATTN_SUB, LANES), 1)
    for s in range(tq // ATTN_SUB):
        row0 = s * ATTN_SUB
        kpos = i * tq + (row0 - Hh) + c
        mask = (c >= t) & (c <= t + 2 * Hh) & (kpos >= 0) & (kpos < L)
        lse_tile = jnp.zeros((ATTN_SUB, LANES), F32)
        for h in range(ATTN_HEADS):
            cols = slice(h * HEAD_DIM, (h + 1) * HEAD_DIM)
            q = q_ref[row0:row0 + ATTN_SUB, cols]
            k = kc[row0:row0 + nk, cols]
            v = vc[row0:row0 + nk, cols]
            logits = lax.dot_general(q, k, (((1,), (1,)), ((), ())), preferred_element_type=F32)
            logits = jnp.where(mask, logits, -jnp.inf)
            mx = jnp.max(logits, axis=-1, keepdims=True)
            p = jnp.exp(logits - mx)
            l = jnp.sum(p, axis=-1, keepdims=True)
            o = _dot(p.astype(BF16), v) / l
            o_ref[row0:row0 + ATTN_SUB, cols] = o.astype(o_ref.dtype)
            lse_tile = jnp.where(lane_h == h, mx + jnp.log(l), lse_tile)
        lse_ref[row0:row0 + ATTN_SUB, :] = lse_tile


def band_attention(qkv, group, dil, tq=256):
    S, C = qkv.shape
    L = S // dil
    tq = min(tq, L)
    nb = C // HEADS_COLS
    x2 = qkv.reshape(L, dil * C)
    hb = tq // ATTN_HALF
    nhalo = L // ATTN_HALF

    def main(kind):
        return pl.BlockSpec((tq, HEADS_COLS), lambda r, i: (i, r * nb + group * 3 + kind))

    def prev(kind):
        return pl.BlockSpec((ATTN_HALF, HEADS_COLS),
                            lambda r, i: (jnp.maximum(i * hb - 1, 0), r * nb + group * 3 + kind))

    def nxt(kind):
        return pl.BlockSpec((ATTN_HALF, HEADS_COLS),
                            lambda r, i: (jnp.minimum((i + 1) * hb, nhalo - 1), r * nb + group * 3 + kind))

    o, lse = pl.pallas_call(
        functools.partial(_band_attn_body, tq=tq, L=L),
        grid=(dil, L // tq),
        in_specs=[main(0), prev(1), main(1), nxt(1), prev(2), main(2), nxt(2)],
        out_specs=[pl.BlockSpec((tq, HEADS_COLS), lambda r, i: (i, r)),
                   pl.BlockSpec((tq, LANES), lambda r, i: (i, r))],
        out_shape=[_sds((L, dil * HEADS_COLS), BF16), _sds((L, dil * LANES), F32)],
        scratch_shapes=[pltpu.VMEM((tq + 2 * ATTN_HALF, HEADS_COLS), BF16),
                        pltpu.VMEM((tq + 2 * ATTN_HALF, HEADS_COLS), BF16)],
        compiler_params=_cp(("parallel", "parallel")),
        name=f"band_attention_d{dil}",
    )(x2, x2, x2, x2, x2, x2, x2)
    return o.reshape(S, HEADS_COLS), lse.reshape(S, LANES)


def _attn_combine_body(o0_ref, o1_ref, o2_ref, l0_ref, l1_ref, l2_ref, out_ref):
    l0, l1, l2 = l0_ref[...], l1_ref[...], l2_ref[...]
    mx = jnp.maximum(jnp.maximum(l0, l1), l2)
    e0, e1, e2 = jnp.exp(l0 - mx), jnp.exp(l1 - mx), jnp.exp(l2 - mx)
    den = e0 + e1 + e2
    a0, a1, a2 = e0 / den, e1 / den, e2 / den
    for h in range(ATTN_HEADS):
        cols = slice(h * HEAD_DIM, (h + 1) * HEAD_DIM)
        acc = (a0[:, h:h + 1] * o0_ref[:, cols].astype(F32)
               + a1[:, h:h + 1] * o1_ref[:, cols].astype(F32)
               + a2[:, h:h + 1] * o2_ref[:, cols].astype(F32))
        out_ref[:, cols] = acc.astype(out_ref.dtype)


def attn_combine(outs, lses, tm=256):
    S = outs[0].shape[0]
    ospec = pl.BlockSpec((tm, HEADS_COLS), lambda i: (i, 0))
    lspec = pl.BlockSpec((tm, LANES), lambda i: (i, 0))
    return pl.pallas_call(
        _attn_combine_body,
        grid=(S // tm,),
        in_specs=[ospec, ospec, ospec, lspec, lspec, lspec],
        out_specs=ospec,
        out_shape=_sds((S, HEADS_COLS), BF16),
        compiler_params=_cp(("parallel",)),
        name="attn_combine",
    )(*outs, *lses)


def _row_to_col(row, eye):
    return jnp.sum(jnp.where(eye, row, 0.0), axis=1, keepdims=True)


def _mlstm_body(q_ref, k_ref, v_ref, gi_ref, gf_ref, o_ref, C_ref, n_ref, m_ref, *, L):
    rev = pl.program_id(0) % 2

    @pl.when(pl.program_id(1) == 0)
    def _():
        C_ref[...] = jnp.zeros_like(C_ref)
        n_ref[...] = jnp.zeros_like(n_ref)
        m_ref[...] = jnp.zeros_like(m_ref)

    q = q_ref[...] * (MLSTM_DQK ** -0.5)
    k = k_ref[...]
    v = v_ref[...]
    gi = GATE_SOFTCAP * jnp.tanh(gi_ref[...] / GATE_SOFTCAP)
    gf = GATE_SOFTCAP * jnp.tanh(gf_ref[...] / GATE_SOFTCAP)
    lf = -(jnp.maximum(-gf, 0.0) + jnp.log1p(jnp.exp(-jnp.abs(gf))))
    li8 = jnp.broadcast_to(gi, (8, L))
    lf8 = jnp.broadcast_to(lf, (8, L))
    lane = lax.broadcasted_iota(I32, (8, L), 1)
    p = lf8
    d = 1
    while d < L:
        p = p + jnp.where(lane >= d, pltpu.roll(p, d, 1), 0.0)
        d *= 2
    total = p[:, L - 1:L]
    b8 = jnp.where(rev == 0, p, total - p + lf8)
    b_row, li_row = b8[0:1, :], li8[0:1, :]
    b_last = total[0:1, :]

    t_idx = lax.broadcasted_iota(I32, (L, L), 0)
    s_idx = lax.broadcasted_iota(I32, (L, L), 1)
    eye = t_idx == s_idx
    b_col = _row_to_col(b_row, eye)
    li_col = _row_to_col(li_row, eye)
    causal = (s_idx - t_idx) * (1 - 2 * rev) <= 0

    m = m_ref[...]
    C = C_ref[...]
    n = n_ref[...]
    d_intra = jnp.where(causal, b_col + (li_row - b_row), -jnp.inf)
    d_inter = b_col + m
    m_t = jnp.maximum(d_inter, jnp.max(d_intra, axis=1, keepdims=True))
    w_inter = jnp.exp(d_inter - m_t)
    s = lax.dot_general(q, k, (((1,), (1,)), ((), ())), preferred_element_type=F32) * jnp.exp(d_intra - m_t)
    num = w_inter * _dot(q, C.astype(BF16)) + _dot(s.astype(BF16), v)
    den = (w_inter * jnp.sum(q.astype(F32) * n, axis=1, keepdims=True)
           + jnp.sum(s, axis=1, keepdims=True))
    o_ref[...] = num / jnp.maximum(jnp.abs(den), jnp.exp(-m_t))

    a_row = b_last - b_row + li_row
    a_col = b_last - b_col + li_col
    m_new = jnp.maximum(b_last + m, jnp.max(a_row, axis=1, keepdims=True))
    decay = jnp.exp(b_last + m - m_new)
    wk = jnp.exp(a_col - m_new) * k.astype(F32)
    C_ref[...] = decay * C + lax.dot_general(wk.astype(BF16), v, (((0,), (0,)), ((), ())),
                                             preferred_element_type=F32)
    n_ref[...] = decay * n + jnp.sum(wk, axis=0, keepdims=True)
    m_ref[...] = m_new


def mlstm_scan(proj, gates):
    S = proj.shape[0]
    H, L = MLSTM_HEADS, MLSTM_CHUNK
    NC = S // L
    kq, kv = H, (2 * H * MLSTM_DQK) // MLSTM_DV
    g3 = gates.reshape(4 * H, 1, S)

    def chunk(hd, c):
        rev = hd % 2
        return c + rev * (NC - 1 - 2 * c)

    return pl.pallas_call(
        functools.partial(_mlstm_body, L=L),
        grid=(2 * H, NC),
        in_specs=[pl.BlockSpec((L, MLSTM_DQK), lambda hd, c: (chunk(hd, c), hd // 2)),
                  pl.BlockSpec((L, MLSTM_DQK), lambda hd, c: (chunk(hd, c), kq + hd // 2)),
                  pl.BlockSpec((L, MLSTM_DV), lambda hd, c: (chunk(hd, c), kv + hd // 2)),
                  pl.BlockSpec((None, 1, L), lambda hd, c: ((hd % 2) * 2 * H + hd // 2, 0, chunk(hd, c))),
                  pl.BlockSpec((None, 1, L), lambda hd, c: ((hd % 2) * 2 * H + H + hd // 2, 0, chunk(hd, c)))],
        out_specs=pl.BlockSpec((None, L, MLSTM_DV), lambda hd, c: (hd % 2, chunk(hd, c), hd // 2)),
        out_shape=_sds((2, S, H * MLSTM_DV), F32),
        scratch_shapes=[pltpu.VMEM((MLSTM_DQK, MLSTM_DV), F32),
                        pltpu.VMEM((1, MLSTM_DQK), F32),
                        pltpu.VMEM((1, 1), F32)],
        compiler_params=_cp(("parallel", "arbitrary")),
        name="mlstm_scan",
    )(proj, proj, proj, g3, g3)


def _mlstm_out_body(h_ref, o_ref, hn_ref, y_ref):
    hs = h_ref[0] + h_ref[1]
    for h in range(MLSTM_HEADS):
        cols = slice(h * MLSTM_DV, (h + 1) * MLSTM_DV)
        x = hs[:, cols]
        x = x * lax.rsqrt(jnp.mean(x * x, axis=-1, keepdims=True) + RMS_EPS) * hn_ref[:, cols]
        y_ref[:, cols] = (x * jax.nn.sigmoid(o_ref[:, cols].astype(F32))).astype(y_ref.dtype)


def mlstm_out(h2, proj, head_norm, tm=256):
    S, D = h2.shape[1], h2.shape[2]
    ob = (2 * MLSTM_HEADS * MLSTM_DQK + D) // D
    return pl.pallas_call(
        _mlstm_out_body,
        grid=(S // tm,),
        in_specs=[pl.BlockSpec((2, tm, D), lambda i: (0, i, 0)),
                  pl.BlockSpec((tm, D), lambda i: (i, ob)),
                  pl.BlockSpec((1, D), lambda i: (0, 0))],
        out_specs=pl.BlockSpec((tm, D), lambda i: (i, 0)),
        out_shape=_sds((S, D), BF16),
        compiler_params=_cp(("parallel",)),
        name="mlstm_out",
    )(h2, proj, head_norm.reshape(1, D))


def _excl_prefix(mask, U, Ls):
    W = _dot(mask.astype(BF16), U)
    totb = jnp.broadcast_to(W[:, LANES - 1:LANES], W.shape)
    offb = _dot(Ls, totb.astype(BF16))
    return offb + W - mask, W, offb


def _route_body(lg_ref, idx_ref, gate_ref, aff_s, gt_s, eq_s, need_s, *, NCH, cap):
    E = lg_ref.shape[0]
    lg = lg_ref[...]
    mx = jnp.max(lg, axis=0, keepdims=True)
    ex = jnp.exp(lg - mx)
    aff = ex / jnp.sum(ex, axis=0, keepdims=True)
    bits = lax.bitcast_convert_type(aff, I32)

    def count(m):
        return jnp.sum(jnp.sum(m.astype(F32), axis=1, keepdims=True), axis=2, keepdims=True)

    def search(it, T):
        cand = T | lax.shift_left(jnp.int32(1), 30 - it)
        return jnp.where(count(bits >= cand) >= cap, cand, T)

    T = lax.fori_loop(0, 31, search, jnp.zeros((E, 1, 1), I32))
    gt = bits > T
    aff_s[...] = aff
    gt_s[...] = gt.astype(F32)
    eq_s[...] = (bits == T).astype(F32)
    need_s[...] = jnp.broadcast_to(cap - count(gt), need_s.shape)

    r = lax.broadcasted_iota(I32, (LANES, LANES), 0)
    cl = lax.broadcasted_iota(I32, (LANES, LANES), 1)
    U = (r <= cl).astype(BF16)
    r2 = lax.broadcasted_iota(I32, (NCH, NCH), 0)
    c2 = lax.broadcasted_iota(I32, (NCH, NCH), 1)
    Ls = (c2 < r2).astype(BF16)
    j_row = lax.broadcasted_iota(I32, (1, cap), 1).astype(F32)
    c_iota = lax.broadcasted_iota(I32, (NCH, cap), 0).astype(F32)
    l_iota = lax.broadcasted_iota(I32, (LANES, cap), 0).astype(F32)

    def per_expert(e, carry):
        eq = eq_s[e]
        need = need_s[e][0:1, 0:1]
        rank_eq, _, _ = _excl_prefix(eq, U, Ls)
        sel = gt_s[e] + eq * (rank_eq < need).astype(F32)
        _, W, offb = _excl_prefix(sel, U, Ls)
        off_col = offb[:, 0:1]
        incl_col = off_col + W[:, LANES - 1:LANES]
        cidx = jnp.sum((incl_col <= j_row).astype(F32), axis=0, keepdims=True)
        onehot = (c_iota == cidx).astype(F32)
        local = j_row - jnp.sum(off_col * onehot, axis=0, keepdims=True)
        oh = onehot.astype(BF16)
        Wsel = _dot(W.T.astype(BF16), oh)
        lidx = jnp.sum((Wsel <= local).astype(F32), axis=0, keepdims=True)
        idx_ref[e] = (cidx * LANES + lidx).astype(I32)
        at = aff_s[e].T
        hi = at.astype(BF16)
        r1 = at - hi.astype(F32)
        mid = r1.astype(BF16)
        lo = (r1 - mid.astype(F32)).astype(BF16)
        asel = _dot(hi, oh) + _dot(mid, oh) + _dot(lo, oh)
        gate_ref[e] = jnp.sum(jnp.where(l_iota == lidx, asel, 0.0), axis=0, keepdims=True)
        return carry

    lax.fori_loop(0, E, per_expert, 0)


def route(logits_t, cap):
    E, S = logits_t.shape
    NCH = S // LANES
    full = pl.BlockSpec((E, NCH, LANES), lambda: (0, 0, 0))
    ospec = pl.BlockSpec((E, 1, cap), lambda: (0, 0, 0))
    return pl.pallas_call(
        functools.partial(_route_body, NCH=NCH, cap=cap),
        in_specs=[full],
        out_specs=[ospec, ospec],
        out_shape=[_sds((E, 1, cap), I32), _sds((E, 1, cap), F32)],
        scratch_shapes=[pltpu.VMEM((E, NCH, LANES), F32), pltpu.VMEM((E, NCH, LANES), F32),
                        pltpu.VMEM((E, NCH, LANES), F32), pltpu.VMEM((E, 8, LANES), F32)],
        compiler_params=pltpu.CompilerParams(vmem_limit_bytes=VMEM_LIMIT),
        name="route",
    )(logits_t.reshape(E, NCH, LANES))


def _issue_row_gather(idx_ref, src_hbm, dst, sem, tm):
    def body(r, carry):
        row = idx_ref[0, 0, r]
        pltpu.make_async_copy(src_hbm.at[pl.ds(row, 1)], dst.at[pl.ds(r, 1)], sem).start()
        return carry
    lax.fori_loop(0, tm, body, 0, unroll=8)


def _moe_up_body(idx_ref, idxn_ref, hn_hbm, wg_ref, wu_ref, o_ref, buf, sem, *, tm):
    nt = pl.num_programs(1)
    step = pl.program_id(0) * nt + pl.program_id(1)
    nsteps = pl.num_programs(0) * nt
    slot = step % 2

    @pl.when(step == 0)
    def _():
        _issue_row_gather(idx_ref, hn_hbm, buf.at[0], sem.at[0], tm)

    @pl.when(step + 1 < nsteps)
    def _():
        _issue_row_gather(idxn_ref, hn_hbm, buf.at[1 - slot], sem.at[1 - slot], tm)

    pltpu.make_async_copy(hn_hbm.at[pl.ds(0, tm)], buf.at[slot], sem.at[slot]).wait()
    xb = buf[slot].astype(BF16)
    g = _dot(xb, wg_ref[...])
    u = _dot(xb, wu_ref[...])
    o_ref[...] = (g * jax.nn.sigmoid(g) * u).astype(o_ref.dtype)


def moe_up(hn, idx, wg, wu, tm=256):
    E, _, cap = idx.shape
    D = hn.shape[1]
    F = wg.shape[2]
    nt = cap // tm
    nsteps = E * nt

    def nxt(e, t):
        s = jnp.minimum(e * nt + t + 1, nsteps - 1)
        return (s // nt, 0, s % nt)

    return pl.pallas_call(
        functools.partial(_moe_up_body, tm=tm),
        grid=(E, nt),
        in_specs=[pl.BlockSpec((1, 1, tm), lambda e, t: (e, 0, t), memory_space=pltpu.SMEM),
                  pl.BlockSpec((1, 1, tm), nxt, memory_space=pltpu.SMEM),
                  pl.BlockSpec(memory_space=pl.ANY),
                  pl.BlockSpec((None, D, F), lambda e, t: (e, 0, 0)),
                  pl.BlockSpec((None, D, F), lambda e, t: (e, 0, 0))],
        out_specs=pl.BlockSpec((tm, F), lambda e, t: (e * nt + t, 0)),
        out_shape=_sds((E * cap, F), BF16),
        scratch_shapes=[pltpu.VMEM((2, tm, D), F32), pltpu.SemaphoreType.DMA((2,))],
        compiler_params=_cp(("arbitrary", "arbitrary")),
        name="moe_up",
    )(idx, idx, hn, wg, wu)


def _moe_down_body(idx_ref, gate_ref, h_ref, wd_ref, x_hbm, xo_hbm, xbuf, gsem, ssem, *, tm):
    del x_hbm
    nt = pl.num_programs(1)
    step = pl.program_id(0) * nt + pl.program_id(1)
    nsteps = pl.num_programs(0) * nt

    def wait_scatter():
        pltpu.make_async_copy(xbuf, xo_hbm.at[pl.ds(0, tm)], ssem).wait()

    @pl.when(step > 0)
    def _():
        wait_scatter()

    _issue_row_gather(idx_ref, xo_hbm, xbuf, gsem, tm)
    y = _dot(h_ref[...], wd_ref[...])
    t_idx = lax.broadcasted_iota(I32, (tm, tm), 0)
    s_idx = lax.broadcasted_iota(I32, (tm, tm), 1)
    gate_col = _row_to_col(gate_ref[0], t_idx == s_idx)
    pltpu.make_async_copy(xo_hbm.at[pl.ds(0, tm)], xbuf, gsem).wait()
    xbuf[...] = xbuf[...] + y * gate_col

    def body(r, carry):
        row = idx_ref[0, 0, r]
        pltpu.make_async_copy(xbuf.at[pl.ds(r, 1)], xo_hbm.at[pl.ds(row, 1)], ssem).start()
        return carry
    lax.fori_loop(0, tm, body, 0, unroll=8)

    @pl.when(step == nsteps - 1)
    def _():
        wait_scatter()


def moe_down(x, hmid, idx, gate, wd, tm=256):
    E, _, cap = idx.shape
    S, D = x.shape
    F = wd.shape[1]
    nt = cap // tm
    return pl.pallas_call(
        functools.partial(_moe_down_body, tm=tm),
        grid=(E, nt),
        in_specs=[pl.BlockSpec((1, 1, tm), lambda e, t: (e, 0, t), memory_space=pltpu.SMEM),
                  pl.BlockSpec((None, 1, tm), lambda e, t: (e, 0, t)),
                  pl.BlockSpec((tm, F), lambda e, t: (e * nt + t, 0)),
                  pl.BlockSpec((None, F, D), lambda e, t: (e, 0, 0)),
                  pl.BlockSpec(memory_space=pl.ANY)],
        out_specs=pl.BlockSpec(memory_space=pl.ANY),
        out_shape=_sds((S, D), F32),
        input_output_aliases={4: 0},
        scratch_shapes=[pltpu.VMEM((tm, D), F32), pltpu.SemaphoreType.DMA(()),
                        pltpu.SemaphoreType.DMA(())],
        compiler_params=_cp(("arbitrary", "arbitrary")),
        name="moe_down",
    )(idx, gate, hmid, wd, x)


def expert_choice_ffn(x, norm_g, w_router, w_gate, w_up, w_down):
    S = x.shape[0]
    E = w_router.shape[1]
    cap = CAPACITY_FACTOR * S // E
    hn, logits_t = rmsnorm_proj(x, norm_g, w_router, jnp.zeros((E,), F32), F32)
    idx, gate = route(logits_t, cap)
    hmid = moe_up(hn, idx, w_gate.astype(BF16), w_up.astype(BF16))
    return moe_down(x, hmid, idx, gate, w_down.astype(BF16))


def conformer_mixer(x, norm_g, w_in, b_in, w_dw, b_dw, ln_g, ln_b, w_out, b_out):
    h = rmsnorm(x, norm_g, BF16)
    u = matmul_glu(h, w_in.astype(BF16), b_in)
    u = dwconv_ln_silu(u, w_dw, b_dw, ln_g, ln_b)
    return matmul_bias_residual(u, w_out.astype(BF16), b_out, x)


def attention_mixer(x, norm_g, positions, w_qkv, w_out):
    h = rmsnorm(x, norm_g, BF16)
    qkv = matmul_qkv_rope(h, w_qkv.astype(BF16), rope_tables(positions))
    outs, lses = [], []
    for g, (_, dil) in enumerate(DIL_PATTERNS):
        o, lse = band_attention(qkv, g, dil)
        outs.append(o)
        lses.append(lse)
    o = attn_combine(outs, lses)
    return matmul_bias_residual(o, w_out.astype(BF16), jnp.zeros((w_out.shape[1],), F32), x)


def mlstm_mixer(x, norm_g, w_in, b_gates, head_norm, w_out):
    D = x.shape[1]
    n_main = 2 * MLSTM_HEADS * MLSTM_DQK + 2 * D
    h, gates = rmsnorm_proj(x, norm_g, w_in[:, n_main:], b_gates, BF16)
    proj = matmul_plain(h, w_in[:, :n_main].astype(BF16), BF16)
    h2 = mlstm_scan(proj, gates)
    y = mlstm_out(h2, proj, head_norm)
    return matmul_bias_residual(y, w_out.astype(BF16), jnp.zeros((D,), F32), x)


def kernel(x, positions, mix_norm, ffn_norm, conv_w_in, conv_b_in, conv_w_dw, conv_b_dw, conv_ln_g,
           conv_ln_b, conv_w_out, conv_b_out, attn_w_qkv, attn_w_out, mlstm_w_in, mlstm_b_gates,
           mlstm_head_norm, mlstm_w_out, router_w, moe_w_gate, moe_w_up, moe_w_down, final_norm):
    assert x.shape[0] == 1, "single-sequence batch"
    xs = x[0]
    pos = positions[0]
    depth = mix_norm.shape[0]
    for i in range(depth):
        kind, j = i % 3, i // 3
        if kind == 0:
            xs = conformer_mixer(xs, mix_norm[i], conv_w_in[j], conv_b_in[j], conv_w_dw[j], conv_b_dw[j],
                                 conv_ln_g[j], conv_ln_b[j], conv_w_out[j], conv_b_out[j])
        elif kind == 1:
            xs = attention_mixer(xs, mix_norm[i], pos, attn_w_qkv[j], attn_w_out[j])
        else:
            xs = mlstm_mixer(xs, mix_norm[i], mlstm_w_in[j], mlstm_b_gates[j], mlstm_head_norm[j],
                             mlstm_w_out[j])
        xs = expert_choice_ffn(xs, ffn_norm[i], router_w[i], moe_w_gate[i], moe_w_up[i], moe_w_down[i])
    return rmsnorm(xs, final_norm, F32)[None]
```

```python
import functools

import jax
import jax.numpy as jnp
from jax import lax
from jax.experimental import pallas as pl
from jax.experimental.pallas import tpu as pltpu

F32 = jnp.float32
BF16 = jnp.bfloat16
I32 = jnp.int32

RMS_EPS = 1e-6
CONV_WIDTH = 31
CONV_HALO = 16
HEAD_DIM = 128
ATTN_HEADS = 16
DIL_PATTERNS = ((128, 1), (512, 4), (2048, 16))
ATTN_HALF = 64
ROT_DIM = 32
ROPE_THETA = 500000.0
GROUP_COLS = 3 * ATTN_HEADS * HEAD_DIM
HEADS_COLS = ATTN_HEADS * HEAD_DIM
MLSTM_HEADS = 8
MLSTM_DQK = 256
MLSTM_DV = 512
MLSTM_CHUNK = 128
GATE_SOFTCAP = 15.0
N_EXPERTS = 16
MOE_TILE = 512
CAPACITY_FACTOR = 2
LANES = 128
MXU_COLS = 256
VMEM_LIMIT = 56 * 1024 * 1024


def _cp(sem, vmem=VMEM_LIMIT):
    return pltpu.CompilerParams(dimension_semantics=sem, vmem_limit_bytes=vmem)


def _sds(shape, dtype):
    return jax.ShapeDtypeStruct(shape, dtype)


def _rmsnorm_body(x_ref, g_ref, o_ref):
    x = x_ref[...]
    ms = jnp.mean(x * x, axis=-1, keepdims=True)
    o_ref[...] = (x * lax.rsqrt(ms + RMS_EPS) * g_ref[...]).astype(o_ref.dtype)


def rmsnorm(x, g, out_dtype, tm=256):
    S, D = x.shape
    return pl.pallas_call(
        _rmsnorm_body,
        grid=(S // tm,),
        in_specs=[pl.BlockSpec((tm, D), lambda i: (i, 0)),
                  pl.BlockSpec((1, D), lambda i: (0, 0))],
        out_specs=pl.BlockSpec((tm, D), lambda i: (i, 0)),
        out_shape=_sds((S, D), out_dtype),
        compiler_params=_cp(("parallel",)),
        name="rmsnorm",
    )(x, g.reshape(1, D))


def _rmsnorm_proj_body(x_ref, g_ref, wt_ref, b_ref, o_ref, p_ref):
    x = x_ref[...]
    ms = jnp.mean(x * x, axis=-1, keepdims=True)
    y = x * lax.rsqrt(ms + RMS_EPS) * g_ref[...]
    o_ref[...] = y.astype(o_ref.dtype)
    p = lax.dot_general(wt_ref[...], y, (((1,), (1,)), ((), ())),
                        precision=lax.Precision.HIGHEST, preferred_element_type=F32)
    p_ref[...] = p + b_ref[...]


def rmsnorm_proj(x, g, w, b, out_dtype, tm=256):
    S, D = x.shape
    N = w.shape[1]
    return pl.pallas_call(
        _rmsnorm_proj_body,
        grid=(S // tm,),
        in_specs=[pl.BlockSpec((tm, D), lambda i: (i, 0)),
                  pl.BlockSpec((1, D), lambda i: (0, 0)),
                  pl.BlockSpec((N, D), lambda i: (0, 0)),
                  pl.BlockSpec((N, 1), lambda i: (0, 0))],
        out_specs=[pl.BlockSpec((tm, D), lambda i: (i, 0)),
                   pl.BlockSpec((N, tm), lambda i: (0, i))],
        out_shape=[_sds((S, D), out_dtype), _sds((N, S), F32)],
        compiler_params=_cp(("parallel",)),
        name="rmsnorm_proj",
    )(x, g.reshape(1, D), w.T, b.reshape(N, 1))


def _dot(a, b):
    return jnp.dot(a, b, preferred_element_type=F32)


def _mm_plain_body(a_ref, b_ref, o_ref):
    o_ref[...] = _dot(a_ref[...], b_ref[...]).astype(o_ref.dtype)


def matmul_plain(a, b, out_dtype, tm=512, tn=1024):
    M, K = a.shape
    N = b.shape[1]
    return pl.pallas_call(
        _mm_plain_body,
        grid=(N // tn, M // tm),
        in_specs=[pl.BlockSpec((tm, K), lambda j, i: (i, 0)),
                  pl.BlockSpec((K, tn), lambda j, i: (0, j))],
        out_specs=pl.BlockSpec((tm, tn), lambda j, i: (i, j)),
        out_shape=_sds((M, N), out_dtype),
        compiler_params=_cp(("parallel", "parallel")),
        name="matmul_plain",
    )(a, b)


def _mm_res_body(a_ref, b_ref, bias_ref, r_ref, o_ref):
    o_ref[...] = _dot(a_ref[...], b_ref[...]) + bias_ref[...] + r_ref[...]


def matmul_bias_residual(a, b, bias, res, tm=512, tn=1024):
    M, K = a.shape
    N = b.shape[1]
    return pl.pallas_call(
        _mm_res_body,
        grid=(N // tn, M // tm),
        in_specs=[pl.BlockSpec((tm, K), lambda j, i: (i, 0)),
                  pl.BlockSpec((K, tn), lambda j, i: (0, j)),
                  pl.BlockSpec((1, tn), lambda j, i: (0, j)),
                  pl.BlockSpec((tm, tn), lambda j, i: (i, j))],
        out_specs=pl.BlockSpec((tm, tn), lambda j, i: (i, j)),
        out_shape=_sds((M, N), F32),
        compiler_params=_cp(("parallel", "parallel")),
        name="matmul_bias_residual",
    )(a, b, bias.reshape(1, N), res)


def _mm_glu_body(a_ref, b1_ref, b2_ref, bias1_ref, bias2_ref, o_ref):
    a = a_ref[...]
    u1 = _dot(a, b1_ref[...]) + bias1_ref[...]
    u2 = _dot(a, b2_ref[...]) + bias2_ref[...]
    o_ref[...] = u1 * jax.nn.sigmoid(u2)


def matmul_glu(a, b, bias, tm=512, tn=512):
    M, K = a.shape
    N2 = b.shape[1] // 2
    nb = N2 // tn
    bias = bias.reshape(1, 2 * N2)
    return pl.pallas_call(
        _mm_glu_body,
        grid=(nb, M // tm),
        in_specs=[pl.BlockSpec((tm, K), lambda j, i: (i, 0)),
                  pl.BlockSpec((K, tn), lambda j, i: (0, j)),
                  pl.BlockSpec((K, tn), lambda j, i: (0, j + nb)),
                  pl.BlockSpec((1, tn), lambda j, i: (0, j)),
                  pl.BlockSpec((1, tn), lambda j, i: (0, j + nb))],
        out_specs=pl.BlockSpec((tm, tn), lambda j, i: (i, j)),
        out_shape=_sds((M, N2), F32),
        compiler_params=_cp(("parallel", "parallel")),
        name="matmul_glu",
    )(a, b, b, bias, bias)


def _mm_rope_body(a_ref, b_ref, c_ref, s1_ref, s2_ref, o_ref, stage, *, tn, tiles_per_kind, dil):
    kind = pl.program_id(0) // tiles_per_kind
    rot = kind < 2
    scale = jnp.where(kind == 0, HEAD_DIM ** -0.5, 1.0).astype(F32)
    c = jnp.where(rot, c_ref[...], 1.0) * scale
    s1 = jnp.where(rot, s1_ref[...], 0.0) * scale
    s2 = jnp.where(rot, s2_ref[...], 0.0) * scale
    a = a_ref[...]
    rows = a.shape[0] // dil
    for ch in range(tn // MXU_COLS):
        acc = _dot(a, b_ref[:, ch * MXU_COLS:(ch + 1) * MXU_COLS])
        for hh in range(MXU_COLS // HEAD_DIM):
            hb = ch * (MXU_COLS // HEAD_DIM) + hh
            cols = slice(hb * HEAD_DIM, (hb + 1) * HEAD_DIM)
            t = acc[:, hh * HEAD_DIM:(hh + 1) * HEAD_DIM]
            val = (t * c + pltpu.roll(t, ROT_DIM // 2, 1) * s1
                   + pltpu.roll(t, HEAD_DIM - ROT_DIM // 2, 1) * s2)
            if dil == 1:
                o_ref[0, :, cols] = val.astype(o_ref.dtype)
            else:
                stage[hb] = val
                for r in range(dil):
                    o_ref[r, :, cols] = stage[hb, pl.ds(r, rows, stride=dil), :].astype(o_ref.dtype)


def matmul_qkv_rope(a, b, tabs, group, dil, tm=512, tn=1024):
    M, K = a.shape
    c, s1, s2 = tabs
    nj = GROUP_COLS // tn
    tab_spec = pl.BlockSpec((tm, HEAD_DIM), lambda j, i: (i, 0))
    return pl.pallas_call(
        functools.partial(_mm_rope_body, tn=tn, tiles_per_kind=HEADS_COLS // tn, dil=dil),
        grid=(nj, M // tm),
        in_specs=[pl.BlockSpec((tm, K), lambda j, i: (i, 0)),
                  pl.BlockSpec((K, tn), lambda j, i: (0, group * nj + j)),
                  tab_spec, tab_spec, tab_spec],
        out_specs=pl.BlockSpec((dil, tm // dil, tn), lambda j, i: (0, i, j)),
        out_shape=_sds((dil, M // dil, GROUP_COLS), BF16),
        scratch_shapes=[pltpu.VMEM((tn // HEAD_DIM, tm, HEAD_DIM) if dil > 1 else (1, 8, LANES), F32)],
        compiler_params=_cp(("parallel", "parallel")),
        name=f"matmul_qkv_rope_d{dil}",
    )(a, b, c, s1, s2)


def _rope_tab_body(pos_ref, invf_ref, c_ref, s1_ref, s2_ref):
    half = ROT_DIM // 2
    ang = pos_ref[...].astype(F32) * invf_ref[...]
    cos, sin = jnp.cos(ang), jnp.sin(ang)
    lane = lax.broadcasted_iota(I32, ang.shape, 1)
    c_ref[...] = jnp.where(lane < ROT_DIM, cos, 1.0)
    s1_ref[...] = jnp.where((lane >= half) & (lane < ROT_DIM), sin, 0.0)
    s2_ref[...] = jnp.where(lane < half, -sin, 0.0)


def rope_tables(positions, tm=512):
    S = positions.shape[0]
    half = ROT_DIM // 2
    inv_freq = ROPE_THETA ** (-jnp.arange(half, dtype=F32) / half)
    invf = jnp.zeros((1, HEAD_DIM), F32).at[0, :ROT_DIM].set(jnp.tile(inv_freq, 2))
    spec = pl.BlockSpec((tm, HEAD_DIM), lambda i: (i, 0))
    return pl.pallas_call(
        _rope_tab_body,
        grid=(S // tm,),
        in_specs=[pl.BlockSpec((tm, 1), lambda i: (i, 0)),
                  pl.BlockSpec((1, HEAD_DIM), lambda i: (0, 0))],
        out_specs=[spec, spec, spec],
        out_shape=[_sds((S, HEAD_DIM), F32)] * 3,
        compiler_params=_cp(("parallel",)),
        name="rope_tables",
    )(positions.reshape(S, 1), invf)


def _dwconv_ln_body(up_ref, um_ref, un_ref, w_ref, b_ref, g_ref, beta_ref, o_ref, pad, cv, shifted,
                    *, ts, lanes):
    i = pl.program_id(0)
    last = pl.num_programs(0) - 1
    H = CONV_HALO
    D = um_ref.shape[1]
    pad[0:H, :] = jnp.where(i > 0, up_ref[...], 0.0)
    pad[H:H + ts, :] = um_ref[...]
    pad[H + ts:H + ts + H, :] = jnp.where(i < last, un_ref[...], 0.0)
    off = H - CONV_WIDTH // 2
    for l0 in range(0, D, lanes):
        cols = slice(l0, l0 + lanes)
        out = None
        for b in range(8):
            vb = None
            for a in range(-(-(off + CONV_WIDTH) // 8)):
                k = 8 * a + b - off
                if 0 <= k < CONV_WIDTH:
                    term = pad[8 * a:8 * a + ts + 8, cols] * w_ref[k:k + 1, cols]
                    vb = term if vb is None else vb + term
            shifted[b] = vb
            sh = shifted[b, b:b + ts, :]
            out = sh if out is None else out + sh
        cv[:, cols] = out
    u = cv[...] + b_ref[...]
    mu = jnp.mean(u, axis=-1, keepdims=True)
    uc = u - mu
    y = uc * lax.rsqrt(jnp.mean(uc * uc, axis=-1, keepdims=True) + RMS_EPS)
    y = y * g_ref[...] + beta_ref[...]
    o_ref[...] = (y * jax.nn.sigmoid(y)).astype(o_ref.dtype)


def dwconv_ln_silu(u, w, b, g, beta, ts=128, lanes=512):
    S, D = u.shape
    H = CONV_HALO
    nh = ts // H
    vec = pl.BlockSpec((1, D), lambda i: (0, 0))
    return pl.pallas_call(
        functools.partial(_dwconv_ln_body, ts=ts, lanes=lanes),
        grid=(S // ts,),
        in_specs=[pl.BlockSpec((H, D), lambda i: (jnp.maximum(i * nh - 1, 0), 0)),
                  pl.BlockSpec((ts, D), lambda i: (i, 0)),
                  pl.BlockSpec((H, D), lambda i: (jnp.minimum((i + 1) * nh, S // H - 1), 0)),
                  pl.BlockSpec((CONV_WIDTH, D), lambda i: (0, 0)),
                  vec, vec, vec],
        out_specs=pl.BlockSpec((ts, D), lambda i: (i, 0)),
        out_shape=_sds((S, D), BF16),
        scratch_shapes=[pltpu.VMEM((ts + 2 * H, D), F32), pltpu.VMEM((ts, D), F32),
                        pltpu.VMEM((8, ts + 8, lanes), F32)],
        compiler_params=_cp(("parallel",)),
        name="dwconv_ln_silu",
    )(u, u, u, w, b.reshape(1, D), g.reshape(1, D), beta.reshape(1, D))


ATTN_SUB = 128


def _band_attn_body(q_ref, kp_ref, km_ref, kn_ref, vp_ref, vm_ref, vn_ref, o_ref, lse_ref, kc, vc,
                    *, tq, L):
    i = pl.program_id(1)
    Hh = ATTN_HALF
    kc[0:Hh, :] = kp_ref[...]
    kc[Hh:Hh + tq, :] = km_ref[...]
    kc[Hh + tq:Hh + tq + Hh, :] = kn_ref[...]
    vc[0:Hh, :] = vp_ref[...]
    vc[Hh:Hh + tq, :] = vm_ref[...]
    vc[Hh + tq:Hh + tq + Hh, :] = vn_ref[...]
    nk = ATTN_SUB + 2 * Hh
    t = lax.broadcasted_iota(I32, (ATTN_SUB, nk), 0)
    c = lax.broadcasted_iota(I32, (ATTN_SUB, nk), 1)
    lane_h = lax.broadcasted_iota(I32, (ATTN_SUB, LANES), 1)
    for s in range(tq // ATTN_SUB):
        row0 = s * ATTN_SUB
        kpos = i * tq + (row0 - Hh) + c
        mask = (c >= t) & (c <= t + 2 * Hh) & (kpos >= 0) & (kpos < L)
        lse_tile = jnp.zeros((ATTN_SUB, LANES), F32)
        for h in range(ATTN_HEADS):
            cols = slice(h * HEAD_DIM, (h + 1) * HEAD_DIM)
            q = q_ref[row0:row0 + ATTN_SUB, cols]
            k = kc[row0:row0 + nk, cols]
            v = vc[row0:row0 + nk, cols]
            logits = lax.dot_general(q, k, (((1,), (1,)), ((), ())), preferred_element_type=F32)
            logits = jnp.where(mask, logits, -jnp.inf)
            mx = jnp.max(logits, axis=-1, keepdims=True)
            p = jnp.exp(logits - mx)
            l = jnp.sum(p, axis=-1, keepdims=True)
            o = _dot(p.astype(BF16), v) / l
            o_ref[row0:row0 + ATTN_SUB, cols] = o.astype(o_ref.dtype)
            lse_tile = jnp.where(lane_h == h, mx + jnp.log(l), lse_tile)
        lse_ref[row0:row0 + ATTN_SUB, :] = lse_tile


def band_attention(qkv, dil, tq=256):
    _, L, C = qkv.shape
    S = dil * L
    tq = min(tq, L)
    x2 = qkv.reshape(S, C)
    nq = L // tq
    hb = tq // ATTN_HALF
    nhalo = S // ATTN_HALF

    def main(kind):
        return pl.BlockSpec((tq, HEADS_COLS), lambda r, i: (r * nq + i, kind))

    def prev(kind):
        return pl.BlockSpec((ATTN_HALF, HEADS_COLS),
                            lambda r, i: (jnp.maximum((r * nq + i) * hb - 1, 0), kind))

    def nxt(kind):
        return pl.BlockSpec((ATTN_HALF, HEADS_COLS),
                            lambda r, i: (jnp.minimum((r * nq + i + 1) * hb, nhalo - 1), kind))

    o, lse = pl.pallas_call(
        functools.partial(_band_attn_body, tq=tq, L=L),
        grid=(dil, nq),
        in_specs=[main(0), prev(1), main(1), nxt(1), prev(2), main(2), nxt(2)],
        out_specs=[pl.BlockSpec((tq, HEADS_COLS), lambda r, i: (r * nq + i, 0)),
                   pl.BlockSpec((tq, LANES), lambda r, i: (r * nq + i, 0))],
        out_shape=[_sds((S, HEADS_COLS), BF16), _sds((S, LANES), F32)],
        scratch_shapes=[pltpu.VMEM((tq + 2 * ATTN_HALF, HEADS_COLS), BF16),
                        pltpu.VMEM((tq + 2 * ATTN_HALF, HEADS_COLS), BF16)],
        compiler_params=_cp(("parallel", "parallel")),
        name=f"band_attention_d{dil}",
    )(x2, x2, x2, x2, x2, x2, x2)
    return o.reshape(dil, L, HEADS_COLS), lse.reshape(dil, L, LANES)


def _attn_combine_body(o0_ref, o1_ref, o2_ref, l0_ref, l1_ref, l2_ref, out_ref, o_nat, l_nat):
    for g, (o_ref, l_ref) in enumerate(((o1_ref, l1_ref), (o2_ref, l2_ref))):
        dil, rows = o_ref.shape[0], o_ref.shape[1]
        for r in range(dil):
            l_nat[g, pl.ds(r, rows, stride=dil), :] = l_ref[r]
            for h in range(ATTN_HEADS):
                o_nat[g, h, pl.ds(r, rows, stride=dil), :] = (
                    o_ref[r, :, h * HEAD_DIM:(h + 1) * HEAD_DIM].astype(F32))
    l0, l1, l2 = l0_ref[0], l_nat[0], l_nat[1]
    mx = jnp.maximum(jnp.maximum(l0, l1), l2)
    e0, e1, e2 = jnp.exp(l0 - mx), jnp.exp(l1 - mx), jnp.exp(l2 - mx)
    den = e0 + e1 + e2
    a0, a1, a2 = e0 / den, e1 / den, e2 / den
    for h in range(ATTN_HEADS):
        cols = slice(h * HEAD_DIM, (h + 1) * HEAD_DIM)
        acc = (a0[:, h:h + 1] * o0_ref[0, :, cols].astype(F32)
               + a1[:, h:h + 1] * o_nat[0, h]
               + a2[:, h:h + 1] * o_nat[1, h])
        out_ref[:, cols] = acc.astype(out_ref.dtype)


def attn_combine(outs, lses, tm=256):
    S = outs[0].shape[0] * outs[0].shape[1]

    def spec(a):
        dil = a.shape[0]
        return pl.BlockSpec((dil, tm // dil, a.shape[2]), lambda i: (0, i, 0))

    return pl.pallas_call(
        _attn_combine_body,
        grid=(S // tm,),
        in_specs=[spec(a) for a in (*outs, *lses)],
        out_specs=pl.BlockSpec((tm, HEADS_COLS), lambda i: (i, 0)),
        out_shape=_sds((S, HEADS_COLS), BF16),
        scratch_shapes=[pltpu.VMEM((2, ATTN_HEADS, tm, HEAD_DIM), F32), pltpu.VMEM((2, tm, LANES), F32)],
        compiler_params=_cp(("parallel",)),
        name="attn_combine",
    )(*outs, *lses)


def _row_to_col(row, eye):
    return jnp.sum(jnp.where(eye, row, 0.0), axis=1, keepdims=True)


def _mlstm_gates_body(g_ref, o_ref, *, L, heads):
    x = GATE_SOFTCAP * jnp.tanh(g_ref[...] / GATE_SOFTCAP)
    lf = -(jnp.maximum(-x, 0.0) + jnp.log1p(jnp.exp(-jnp.abs(x))))
    n = x.shape[1]
    pos = lax.broadcasted_iota(I32, x.shape, 1) % L
    pre, suf = lf, lf
    d = 1
    while d < L:
        pre = pre + jnp.where(pos >= d, pltpu.roll(pre, d, 1), 0.0)
        suf = suf + jnp.where(pos < L - d, pltpu.roll(suf, n - d, 1), 0.0)
        d *= 2
    kind = lax.broadcasted_iota(I32, x.shape, 0) // heads
    o_ref[...] = jnp.where(kind == 1, pre, jnp.where(kind == 3, suf, x))


def mlstm_gates(gates, ts=2048):
    R, S = gates.shape
    spec = pl.BlockSpec((R, ts), lambda i: (0, i))
    return pl.pallas_call(
        functools.partial(_mlstm_gates_body, L=MLSTM_CHUNK, heads=MLSTM_HEADS),
        grid=(S // ts,),
        in_specs=[spec],
        out_specs=spec,
        out_shape=_sds((R, S), F32),
        compiler_params=_cp(("parallel",)),
        name="mlstm_gates",
    )(gates)


def _mlstm_chunk(q, k, v, li_row, b_row, C_ref, n_ref, m_ref, rev, L):
    q = q * (MLSTM_DQK ** -0.5)
    b_last = b_row[:, 0:1] if rev else b_row[:, L - 1:L]

    t_idx = lax.broadcasted_iota(I32, (L, L), 0)
    s_idx = lax.broadcasted_iota(I32, (L, L), 1)
    eye = t_idx == s_idx
    b_col = _row_to_col(b_row, eye)
    li_col = _row_to_col(li_row, eye)
    causal = s_idx >= t_idx if rev else s_idx <= t_idx

    m = m_ref[...]
    C = C_ref[...]
    n = n_ref[...]
    d_intra = jnp.where(causal, b_col + (li_row - b_row), -jnp.inf)
    d_inter = b_col + m
    m_t = jnp.maximum(d_inter, jnp.max(d_intra, axis=1, keepdims=True))
    w_inter = jnp.exp(d_inter - m_t)
    s = lax.dot_general(q, k, (((1,), (1,)), ((), ())), preferred_element_type=F32) * jnp.exp(d_intra - m_t)
    num = w_inter * _dot(q, C.astype(BF16)) + _dot(s.astype(BF16), v)
    den = (w_inter * jnp.sum(q.astype(F32) * n, axis=1, keepdims=True)
           + jnp.sum(s, axis=1, keepdims=True))
    h = num / jnp.maximum(jnp.abs(den), jnp.exp(-m_t))

    a_row = b_last - b_row + li_row
    a_col = b_last - b_col + li_col
    m_new = jnp.maximum(b_last + m, jnp.max(a_row, axis=1, keepdims=True))
    decay = jnp.exp(b_last + m - m_new)
    wk = jnp.exp(a_col - m_new) * k.astype(F32)
    C_ref[...] = decay * C + lax.dot_general(wk.astype(BF16), v, (((0,), (0,)), ((), ())),
                                             preferred_element_type=F32)
    n_ref[...] = decay * n + jnp.sum(wk, axis=0, keepdims=True)
    m_ref[...] = m_new
    return h


def _mlstm_body(qf_ref, kf_ref, vf_ref, gif_ref, gff_ref, qb_ref, kb_ref, vb_ref, gib_ref, gfb_ref,
                of_ref, ob_ref, C_ref, n_ref, m_ref, *, L, heads):
    @pl.when(pl.program_id(1) == 0)
    def _():
        C_ref[...] = jnp.zeros_like(C_ref)
        n_ref[...] = jnp.zeros_like(n_ref)
        m_ref[...] = jnp.zeros_like(m_ref)

    for hh in range(heads):
        qk = slice(hh * MLSTM_DQK, (hh + 1) * MLSTM_DQK)
        vv = slice(hh * MLSTM_DV, (hh + 1) * MLSTM_DV)
        for rev, (q_ref, k_ref, v_ref, gi_ref, gf_ref, o_ref) in enumerate(
                ((qf_ref, kf_ref, vf_ref, gif_ref, gff_ref, of_ref),
                 (qb_ref, kb_ref, vb_ref, gib_ref, gfb_ref, ob_ref))):
            st = 2 * hh + rev
            o_ref[:, vv] = _mlstm_chunk(q_ref[:, qk], k_ref[:, qk], v_ref[:, vv], gi_ref[hh], gf_ref[hh],
                                        C_ref.at[st], n_ref.at[st], m_ref.at[st], bool(rev), L)


def mlstm_scan(proj, gates, heads=4):
    S = proj.shape[0]
    H, L = MLSTM_HEADS, MLSTM_CHUNK
    NC = S // L
    HB = H // heads
    kq = HB
    kv = (2 * H * MLSTM_DQK) // (heads * MLSTM_DV)
    g3 = gates.reshape(4 * H, 1, S)

    def specs(rev):
        def ch(c):
            return NC - 1 - c if rev else c
        gi_blk, gf_blk = 2 * rev * HB, (2 * rev + 1) * HB
        return [pl.BlockSpec((L, heads * MLSTM_DQK), lambda p, c: (ch(c), p)),
                pl.BlockSpec((L, heads * MLSTM_DQK), lambda p, c: (ch(c), kq + p)),
                pl.BlockSpec((L, heads * MLSTM_DV), lambda p, c: (ch(c), kv + p)),
                pl.BlockSpec((heads, 1, L), lambda p, c: (gi_blk + p, 0, ch(c))),
                pl.BlockSpec((heads, 1, L), lambda p, c: (gf_blk + p, 0, ch(c)))]

    return pl.pallas_call(
        functools.partial(_mlstm_body, L=L, heads=heads),
        grid=(HB, NC),
        in_specs=specs(0) + specs(1),
        out_specs=[pl.BlockSpec((L, heads * MLSTM_DV), lambda p, c: (c, p)),
                   pl.BlockSpec((L, heads * MLSTM_DV), lambda p, c: (NC - 1 - c, p))],
        out_shape=[_sds((S, H * MLSTM_DV), F32), _sds((S, H * MLSTM_DV), F32)],
        scratch_shapes=[pltpu.VMEM((2 * heads, MLSTM_DQK, MLSTM_DV), F32),
                        pltpu.VMEM((2 * heads, 1, MLSTM_DQK), F32),
                        pltpu.VMEM((2 * heads, 1, 1), F32)],
        compiler_params=_cp(("parallel", "arbitrary")),
        name="mlstm_scan",
    )(proj, proj, proj, g3, g3, proj, proj, proj, g3, g3)


def _mlstm_out_body(hf_ref, hb_ref, o_ref, hn_ref, y_ref):
    hs = hf_ref[...] + hb_ref[...]
    for h in range(MLSTM_HEADS):
        cols = slice(h * MLSTM_DV, (h + 1) * MLSTM_DV)
        x = hs[:, cols]
        x = x * lax.rsqrt(jnp.mean(x * x, axis=-1, keepdims=True) + RMS_EPS) * hn_ref[:, cols]
        y_ref[:, cols] = (x * jax.nn.sigmoid(o_ref[:, cols].astype(F32))).astype(y_ref.dtype)


def mlstm_out(hf, hb, proj, head_norm, tm=256):
    S, D = hf.shape
    ob = (2 * MLSTM_HEADS * MLSTM_DQK + D) // D
    return pl.pallas_call(
        _mlstm_out_body,
        grid=(S // tm,),
        in_specs=[pl.BlockSpec((tm, D), lambda i: (i, 0)),
                  pl.BlockSpec((tm, D), lambda i: (i, 0)),
                  pl.BlockSpec((tm, D), lambda i: (i, ob)),
                  pl.BlockSpec((1, D), lambda i: (0, 0))],
        out_specs=pl.BlockSpec((tm, D), lambda i: (i, 0)),
        out_shape=_sds((S, D), BF16),
        compiler_params=_cp(("parallel",)),
        name="mlstm_out",
    )(hf, hb, proj, head_norm.reshape(1, D))


def _excl_prefix(mask, U, Ls):
    W = _dot(mask.astype(BF16), U)
    totb = jnp.broadcast_to(W[:, LANES - 1:LANES], W.shape)
    offb = _dot(Ls, totb.astype(BF16))
    return offb + W - mask, W, offb


def _route_body(lg_ref, idx_ref, gate_ref, aff_s, gt_s, eq_s, need_s, *, NCH, cap):
    E = lg_ref.shape[0]
    lg = lg_ref[...]
    mx = jnp.max(lg, axis=0, keepdims=True)
    ex = jnp.exp(lg - mx)
    aff = ex / jnp.sum(ex, axis=0, keepdims=True)
    bits = lax.bitcast_convert_type(aff, I32)

    def count(m):
        return jnp.sum(jnp.sum(m.astype(F32), axis=1, keepdims=True), axis=2, keepdims=True)

    def search(it, T):
        cand = T | lax.shift_left(jnp.int32(1), 30 - it)
        return jnp.where(count(bits >= cand) >= cap, cand, T)

    T = lax.fori_loop(0, 31, search, jnp.zeros((E, 1, 1), I32))
    gt = bits > T
    aff_s[...] = aff
    gt_s[...] = gt.astype(F32)
    eq_s[...] = (bits == T).astype(F32)
    need_s[...] = jnp.broadcast_to(cap - count(gt), need_s.shape)

    r = lax.broadcasted_iota(I32, (LANES, LANES), 0)
    cl = lax.broadcasted_iota(I32, (LANES, LANES), 1)
    U = (r <= cl).astype(BF16)
    r2 = lax.broadcasted_iota(I32, (NCH, NCH), 0)
    c2 = lax.broadcasted_iota(I32, (NCH, NCH), 1)
    Ls = (c2 < r2).astype(BF16)
    j_row = lax.broadcasted_iota(I32, (1, cap), 1).astype(F32)
    c_iota = lax.broadcasted_iota(I32, (NCH, cap), 0).astype(F32)
    l_iota = lax.broadcasted_iota(I32, (LANES, cap), 0).astype(F32)

    def per_expert(e, carry):
        eq = eq_s[e]
        need = need_s[e][0:1, 0:1]
        rank_eq, _, _ = _excl_prefix(eq, U, Ls)
        sel = gt_s[e] + eq * (rank_eq < need).astype(F32)
        _, W, offb = _excl_prefix(sel, U, Ls)
        off_col = offb[:, 0:1]
        incl_col = off_col + W[:, LANES - 1:LANES]
        cidx = jnp.sum((incl_col <= j_row).astype(F32), axis=0, keepdims=True)
        onehot = (c_iota == cidx).astype(F32)
        local = j_row - jnp.sum(off_col * onehot, axis=0, keepdims=True)
        oh = onehot.astype(BF16)
        Wsel = _dot(W.T.astype(BF16), oh)
        lidx = jnp.sum((Wsel <= local).astype(F32), axis=0, keepdims=True)
        idx_ref[e] = (cidx * LANES + lidx).astype(I32)
        at = aff_s[e].T
        hi = at.astype(BF16)
        r1 = at - hi.astype(F32)
        mid = r1.astype(BF16)
        lo = (r1 - mid.astype(F32)).astype(BF16)
        asel = _dot(hi, oh) + _dot(mid, oh) + _dot(lo, oh)
        gate_ref[e] = jnp.sum(jnp.where(l_iota == lidx, asel, 0.0), axis=0, keepdims=True)
        return carry

    lax.fori_loop(0, E, per_expert, 0)


def route(logits_t, cap):
    E, S = logits_t.shape
    NCH = S // LANES
    full = pl.BlockSpec((E, NCH, LANES), lambda: (0, 0, 0))
    ospec = pl.BlockSpec((E, 1, cap), lambda: (0, 0, 0))
    return pl.pallas_call(
        functools.partial(_route_body, NCH=NCH, cap=cap),
        in_specs=[full],
        out_specs=[ospec, ospec],
        out_shape=[_sds((E, 1, cap), I32), _sds((E, 1, cap), F32)],
        scratch_shapes=[pltpu.VMEM((E, NCH, LANES), F32), pltpu.VMEM((E, NCH, LANES), F32),
                        pltpu.VMEM((E, NCH, LANES), F32), pltpu.VMEM((E, 8, LANES), F32)],
        compiler_params=pltpu.CompilerParams(vmem_limit_bytes=VMEM_LIMIT),
        name="route",
    )(logits_t.reshape(E, NCH, LANES))


SUBLANES = 8


def _issue_row_copies(idx_ref, tm, make_copy):
    def body(g, carry):
        base = pl.multiple_of(g * SUBLANES, SUBLANES)
        for k in range(SUBLANES):
            make_copy(base + k, idx_ref[0, 0, base + k]).start()
        return carry
    lax.fori_loop(0, tm // SUBLANES, body, 0)


def _issue_row_gather(idx_ref, src_hbm, dst, sem, tm):
    _issue_row_copies(idx_ref, tm, lambda r, row: pltpu.make_async_copy(
        src_hbm.at[pl.ds(row, 1)], dst.at[pl.ds(r, 1)], sem))


def _moe_up_body(idx_ref, idxn_ref, hn_hbm, wg_ref, wu_ref, o_ref, buf, sem, *, tm):
    nt = pl.num_programs(1)
    step = pl.program_id(0) * nt + pl.program_id(1)
    nsteps = pl.num_programs(0) * nt
    slot = step % 2

    @pl.when(step == 0)
    def _():
        _issue_row_gather(idx_ref, hn_hbm, buf.at[0], sem.at[0], tm)

    @pl.when(step + 1 < nsteps)
    def _():
        _issue_row_gather(idxn_ref, hn_hbm, buf.at[1 - slot], sem.at[1 - slot], tm)

    pltpu.make_async_copy(hn_hbm.at[pl.ds(0, tm)], buf.at[slot], sem.at[slot]).wait()
    xb = buf[slot].astype(BF16)
    g = _dot(xb, wg_ref[...])
    u = _dot(xb, wu_ref[...])
    o_ref[...] = (g * jax.nn.sigmoid(g) * u).astype(o_ref.dtype)


def moe_up(hn, idx, wg, wu, tm):
    E, _, cap = idx.shape
    D = hn.shape[1]
    F = wg.shape[2]
    nt = cap // tm
    nsteps = E * nt

    def nxt(e, t):
        s = jnp.minimum(e * nt + t + 1, nsteps - 1)
        return (s // nt, 0, s % nt)

    return pl.pallas_call(
        functools.partial(_moe_up_body, tm=tm),
        grid=(E, nt),
        in_specs=[pl.BlockSpec((1, 1, tm), lambda e, t: (e, 0, t), memory_space=pltpu.SMEM),
                  pl.BlockSpec((1, 1, tm), nxt, memory_space=pltpu.SMEM),
                  pl.BlockSpec(memory_space=pl.ANY),
                  pl.BlockSpec((None, D, F), lambda e, t: (e, 0, 0)),
                  pl.BlockSpec((None, D, F), lambda e, t: (e, 0, 0))],
        out_specs=pl.BlockSpec((tm, F), lambda e, t: (e * nt + t, 0)),
        out_shape=_sds((E * cap, F), BF16),
        scratch_shapes=[pltpu.VMEM((2, tm, D), F32), pltpu.SemaphoreType.DMA((2,))],
        compiler_params=_cp(("arbitrary", "arbitrary")),
        name="moe_up",
    )(idx, idx, hn, wg, wu)


DOWN_SLOTS = 3


def _moe_down_body(idx_ref, idxn_ref, gate_ref, h_ref, wd_ref, x_hbm, xo_hbm, xbuf, gsem, ssem, *, tm):
    del x_hbm
    nt = pl.num_programs(1)
    t = pl.program_id(1)
    step = pl.program_id(0) * nt + t
    nsteps = pl.num_programs(0) * nt
    cur = step % DOWN_SLOTS
    nxt = (step + 1) % DOWN_SLOTS
    prv = (step + 2) % DOWN_SLOTS

    def wait_scatter(slot):
        pltpu.make_async_copy(xbuf.at[slot], xo_hbm.at[pl.ds(0, tm)], ssem.at[slot]).wait()

    @pl.when((step >= 2) & (t != 1))
    def _():
        wait_scatter(nxt)

    @pl.when((t == 0) & (step >= 1))
    def _():
        wait_scatter(prv)

    @pl.when(t == 0)
    def _():
        _issue_row_gather(idx_ref, xo_hbm, xbuf.at[cur], gsem.at[cur], tm)

    @pl.when(t + 1 < nt)
    def _():
        _issue_row_gather(idxn_ref, xo_hbm, xbuf.at[nxt], gsem.at[nxt], tm)

    y = _dot(h_ref[...], wd_ref[...])
    t_idx = lax.broadcasted_iota(I32, (tm, tm), 0)
    s_idx = lax.broadcasted_iota(I32, (tm, tm), 1)
    gate_col = _row_to_col(gate_ref[0], t_idx == s_idx)
    pltpu.make_async_copy(xo_hbm.at[pl.ds(0, tm)], xbuf.at[cur], gsem.at[cur]).wait()
    xbuf[cur] = xbuf[cur] + y * gate_col
    _issue_row_copies(idx_ref, tm, lambda r, row: pltpu.make_async_copy(
        xbuf.at[cur, pl.ds(r, 1)], xo_hbm.at[pl.ds(row, 1)], ssem.at[cur]))

    @pl.when(step == nsteps - 1)
    def _():
        wait_scatter(prv)
        wait_scatter(cur)


def moe_down(x, hmid, idx, gate, wd, tm):
    E, _, cap = idx.shape
    S, D = x.shape
    F = wd.shape[1]
    nt = cap // tm
    assert nt >= 2, "the scatter wait schedule needs at least two tiles per expert"
    nsteps = E * nt

    def nxt(e, t):
        s = jnp.minimum(e * nt + t + 1, nsteps - 1)
        return (s // nt, 0, s % nt)

    return pl.pallas_call(
        functools.partial(_moe_down_body, tm=tm),
        grid=(E, nt),
        in_specs=[pl.BlockSpec((1, 1, tm), lambda e, t: (e, 0, t), memory_space=pltpu.SMEM),
                  pl.BlockSpec((1, 1, tm), nxt, memory_space=pltpu.SMEM),
                  pl.BlockSpec((None, 1, tm), lambda e, t: (e, 0, t)),
                  pl.BlockSpec((tm, F), lambda e, t: (e * nt + t, 0)),
                  pl.BlockSpec((None, F, D), lambda e, t: (e, 0, 0)),
                  pl.BlockSpec(memory_space=pl.ANY)],
        out_specs=pl.BlockSpec(memory_space=pl.ANY),
        out_shape=_sds((S, D), F32),
        input_output_aliases={5: 0},
        scratch_shapes=[pltpu.VMEM((DOWN_SLOTS, tm, D), F32), pltpu.SemaphoreType.DMA((DOWN_SLOTS,)),
                        pltpu.SemaphoreType.DMA((DOWN_SLOTS,))],
        compiler_params=_cp(("arbitrary", "arbitrary")),
        name="moe_down",
    )(idx, idx, gate, hmid, wd, x)


def expert_choice_ffn(x, norm_g, w_router, w_gate, w_up, w_down):
    S = x.shape[0]
    E = w_router.shape[1]
    cap = CAPACITY_FACTOR * S // E
    hn, logits_t = rmsnorm_proj(x, norm_g, w_router, jnp.zeros((E,), F32), F32)
    idx, gate = route(logits_t, cap)
    tm = min(MOE_TILE, cap // 2)
    hmid = moe_up(hn, idx, w_gate.astype(BF16), w_up.astype(BF16), tm)
    return moe_down(x, hmid, idx, gate, w_down.astype(BF16), tm)


def conformer_mixer(x, norm_g, w_in, b_in, w_dw, b_dw, ln_g, ln_b, w_out, b_out):
    h = rmsnorm(x, norm_g, BF16)
    u = matmul_glu(h, w_in.astype(BF16), b_in)
    u = dwconv_ln_silu(u, w_dw, b_dw, ln_g, ln_b)
    return matmul_bias_residual(u, w_out.astype(BF16), b_out, x)


def attention_mixer(x, norm_g, positions, w_qkv, w_out):
    h = rmsnorm(x, norm_g, BF16)
    w = w_qkv.astype(BF16)
    tabs = rope_tables(positions)
    outs, lses = [], []
    for g, (window, dil) in enumerate(DIL_PATTERNS):
        assert window // (2 * dil) == ATTN_HALF
        o, lse = band_attention(matmul_qkv_rope(h, w, tabs, g, dil), dil)
        outs.append(o)
        lses.append(lse)
    o = attn_combine(outs, lses)
    return matmul_bias_residual(o, w_out.astype(BF16), jnp.zeros((w_out.shape[1],), F32), x)


def mlstm_mixer(x, norm_g, w_in, b_gates, head_norm, w_out):
    D = x.shape[1]
    n_main = 2 * MLSTM_HEADS * MLSTM_DQK + 2 * D
    h, gates = rmsnorm_proj(x, norm_g, w_in[:, n_main:], b_gates, BF16)
    proj = matmul_plain(h, w_in[:, :n_main].astype(BF16), BF16)
    hf, hb = mlstm_scan(proj, mlstm_gates(gates))
    y = mlstm_out(hf, hb, proj, head_norm)
    return matmul_bias_residual(y, w_out.astype(BF16), jnp.zeros((D,), F32), x)


def kernel(x, positions, mix_norm, ffn_norm, conv_w_in, conv_b_in, conv_w_dw, conv_b_dw, conv_ln_g,
           conv_ln_b, conv_w_out, conv_b_out, attn_w_qkv, attn_w_out, mlstm_w_in, mlstm_b_gates,
           mlstm_head_norm, mlstm_w_out, router_w, moe_w_gate, moe_w_up, moe_w_down, final_norm):
    assert x.shape[0] == 1, "single-sequence batch"
    xs = x[0]
    pos = positions[0]
    depth = mix_norm.shape[0]
    for i in range(depth):
        kind, j = i % 3, i // 3
        if kind == 0:
            xs = conformer_mixer(xs, mix_norm[i], conv_w_in[j], conv_b_in[j], conv_w_dw[j], conv_b_dw[j],
                                 conv_ln_g[j], conv_ln_b[j], conv_w_out[j], conv_b_out[j])
        elif kind == 1:
            xs = attention_mixer(xs, mix_norm[i], pos, attn_w_qkv[j], attn_w_out[j])
        else:
            xs = mlstm_mixer(xs, mix_norm[i], mlstm_w_in[j], mlstm_b_gates[j], mlstm_head_norm[j],
                             mlstm_w_out[j])
        xs = expert_choice_ffn(xs, ffn_norm[i], router_w[i], moe_w_gate[i], moe_w_up[i], moe_w_down[i])
    return rmsnorm(xs, final_norm, F32)[None]
```

```python
import functools

import jax
import jax.numpy as jnp
from jax import lax
from jax.experimental import pallas as pl
from jax.experimental.pallas import tpu as pltpu

F32 = jnp.float32
BF16 = jnp.bfloat16
I32 = jnp.int32

RMS_EPS = 1e-6
CONV_WIDTH = 31
CONV_HALO = 16
HEAD_DIM = 128
ATTN_HEADS = 16
DIL_PATTERNS = ((128, 1), (512, 4), (2048, 16))
ATTN_HALF = 64
ROT_DIM = 32
ROPE_THETA = 500000.0
GROUP_COLS = 3 * ATTN_HEADS * HEAD_DIM
HEADS_COLS = ATTN_HEADS * HEAD_DIM
MLSTM_HEADS = 8
MLSTM_DQK = 256
MLSTM_DV = 512
MLSTM_CHUNK = 128
GATE_SOFTCAP = 15.0
N_EXPERTS = 16
MOE_TILE = 512
CAPACITY_FACTOR = 2
LANES = 128
MXU_COLS = 256
VMEM_LIMIT = 56 * 1024 * 1024


def _cp(sem, vmem=VMEM_LIMIT):
    return pltpu.CompilerParams(dimension_semantics=sem, vmem_limit_bytes=vmem)


def _sds(shape, dtype):
    return jax.ShapeDtypeStruct(shape, dtype)


def _rmsnorm_body(x_ref, g_ref, o_ref):
    x = x_ref[...]
    ms = jnp.mean(x * x, axis=-1, keepdims=True)
    o_ref[...] = (x * lax.rsqrt(ms + RMS_EPS) * g_ref[...]).astype(o_ref.dtype)


def rmsnorm(x, g, out_dtype, tm=256):
    S, D = x.shape
    return pl.pallas_call(
        _rmsnorm_body,
        grid=(S // tm,),
        in_specs=[pl.BlockSpec((tm, D), lambda i: (i, 0)),
                  pl.BlockSpec((1, D), lambda i: (0, 0))],
        out_specs=pl.BlockSpec((tm, D), lambda i: (i, 0)),
        out_shape=_sds((S, D), out_dtype),
        compiler_params=_cp(("parallel",)),
        name="rmsnorm",
    )(x, g.reshape(1, D))


def _rmsnorm_proj_body(x_ref, g_ref, wt_ref, b_ref, o_ref, p_ref):
    x = x_ref[...]
    ms = jnp.mean(x * x, axis=-1, keepdims=True)
    y = x * lax.rsqrt(ms + RMS_EPS) * g_ref[...]
    o_ref[...] = y.astype(o_ref.dtype)
    p = lax.dot_general(wt_ref[...], y, (((1,), (1,)), ((), ())),
                        precision=lax.Precision.HIGHEST, preferred_element_type=F32)
    p_ref[...] = p + b_ref[...]


def rmsnorm_proj(x, g, w, b, out_dtype, tm=256):
    S, D = x.shape
    N = w.shape[1]
    return pl.pallas_call(
        _rmsnorm_proj_body,
        grid=(S // tm,),
        in_specs=[pl.BlockSpec((tm, D), lambda i: (i, 0)),
                  pl.BlockSpec((1, D), lambda i: (0, 0)),
                  pl.BlockSpec((N, D), lambda i: (0, 0)),
                  pl.BlockSpec((N, 1), lambda i: (0, 0))],
        out_specs=[pl.BlockSpec((tm, D), lambda i: (i, 0)),
                   pl.BlockSpec((N, tm), lambda i: (0, i))],
        out_shape=[_sds((S, D), out_dtype), _sds((N, S), F32)],
        compiler_params=_cp(("parallel",)),
        name="rmsnorm_proj",
    )(x, g.reshape(1, D), w.T, b.reshape(N, 1))


def _dot(a, b):
    return jnp.dot(a, b, preferred_element_type=F32)


def _mm_plain_body(a_ref, b_ref, o_ref):
    o_ref[...] = _dot(a_ref[...], b_ref[...]).astype(o_ref.dtype)


def matmul_plain(a, b, out_dtype, tm=512, tn=1024):
    M, K = a.shape
    N = b.shape[1]
    return pl.pallas_call(
        _mm_plain_body,
        grid=(N // tn, M // tm),
        in_specs=[pl.BlockSpec((tm, K), lambda j, i: (i, 0)),
                  pl.BlockSpec((K, tn), lambda j, i: (0, j))],
        out_specs=pl.BlockSpec((tm, tn), lambda j, i: (i, j)),
        out_shape=_sds((M, N), out_dtype),
        compiler_params=_cp(("parallel", "parallel")),
        name="matmul_plain",
    )(a, b)


def _mm_res_body(a_ref, b_ref, bias_ref, r_ref, o_ref):
    o_ref[...] = _dot(a_ref[...], b_ref[...]) + bias_ref[...] + r_ref[...]


def matmul_bias_residual(a, b, layer, bias, res, tm=512, tn=1024):
    M, K = a.shape
    N = b.shape[2]
    return pl.pallas_call(
        _mm_res_body,
        grid=(N // tn, M // tm),
        in_specs=[pl.BlockSpec((tm, K), lambda j, i: (i, 0)),
                  pl.BlockSpec((None, K, tn), lambda j, i: (layer, 0, j)),
                  pl.BlockSpec((1, tn), lambda j, i: (0, j)),
                  pl.BlockSpec((tm, tn), lambda j, i: (i, j))],
        out_specs=pl.BlockSpec((tm, tn), lambda j, i: (i, j)),
        out_shape=_sds((M, N), F32),
        compiler_params=_cp(("parallel", "parallel")),
        name="matmul_bias_residual",
    )(a, b, bias.reshape(1, N), res)


def _mm_glu_body(a_ref, b1_ref, b2_ref, bias1_ref, bias2_ref, o_ref):
    a = a_ref[...]
    u1 = _dot(a, b1_ref[...]) + bias1_ref[...]
    u2 = _dot(a, b2_ref[...]) + bias2_ref[...]
    o_ref[...] = u1 * jax.nn.sigmoid(u2)


def matmul_glu(a, b, layer, bias, tm=512, tn=512):
    M, K = a.shape
    N2 = b.shape[2] // 2
    nb = N2 // tn
    bias = bias.reshape(1, 2 * N2)
    return pl.pallas_call(
        _mm_glu_body,
        grid=(nb, M // tm),
        in_specs=[pl.BlockSpec((tm, K), lambda j, i: (i, 0)),
                  pl.BlockSpec((None, K, tn), lambda j, i: (layer, 0, j)),
                  pl.BlockSpec((None, K, tn), lambda j, i: (layer, 0, j + nb)),
                  pl.BlockSpec((1, tn), lambda j, i: (0, j)),
                  pl.BlockSpec((1, tn), lambda j, i: (0, j + nb))],
        out_specs=pl.BlockSpec((tm, tn), lambda j, i: (i, j)),
        out_shape=_sds((M, N2), F32),
        compiler_params=_cp(("parallel", "parallel")),
        name="matmul_glu",
    )(a, b, b, bias, bias)


def _mm_rope_body(a_ref, b_ref, c_ref, s1_ref, s2_ref, o_ref, stage, *, tn, tiles_per_kind, dil):
    kind = pl.program_id(0) // tiles_per_kind
    rot = kind < 2
    scale = jnp.where(kind == 0, HEAD_DIM ** -0.5, 1.0).astype(F32)
    c = jnp.where(rot, c_ref[...], 1.0) * scale
    s1 = jnp.where(rot, s1_ref[...], 0.0) * scale
    s2 = jnp.where(rot, s2_ref[...], 0.0) * scale
    a = a_ref[...]
    rows = a.shape[0] // dil
    for ch in range(tn // MXU_COLS):
        acc = _dot(a, b_ref[:, ch * MXU_COLS:(ch + 1) * MXU_COLS])
        for hh in range(MXU_COLS // HEAD_DIM):
            hb = ch * (MXU_COLS // HEAD_DIM) + hh
            cols = slice(hb * HEAD_DIM, (hb + 1) * HEAD_DIM)
            t = acc[:, hh * HEAD_DIM:(hh + 1) * HEAD_DIM]
            val = (t * c + pltpu.roll(t, ROT_DIM // 2, 1) * s1
                   + pltpu.roll(t, HEAD_DIM - ROT_DIM // 2, 1) * s2)
            if dil == 1:
                o_ref[0, :, cols] = val.astype(o_ref.dtype)
            else:
                stage[hb] = val
                for r in range(dil):
                    o_ref[r, :, cols] = stage[hb, pl.ds(r, rows, stride=dil), :].astype(o_ref.dtype)


def matmul_qkv_rope(a, b, tabs, group, dil, tm=512, tn=2048):
    M, K = a.shape
    c, s1, s2 = tabs
    nj = GROUP_COLS // tn
    tab_spec = pl.BlockSpec((tm, HEAD_DIM), lambda j, i: (i, 0))
    return pl.pallas_call(
        functools.partial(_mm_rope_body, tn=tn, tiles_per_kind=HEADS_COLS // tn, dil=dil),
        grid=(nj, M // tm),
        in_specs=[pl.BlockSpec((tm, K), lambda j, i: (i, 0)),
                  pl.BlockSpec((K, tn), lambda j, i: (0, group * nj + j)),
                  tab_spec, tab_spec, tab_spec],
        out_specs=pl.BlockSpec((dil, tm // dil, tn), lambda j, i: (0, i, j)),
        out_shape=_sds((dil, M // dil, GROUP_COLS), BF16),
        scratch_shapes=[pltpu.VMEM((tn // HEAD_DIM, tm, HEAD_DIM) if dil > 1 else (1, 8, LANES), F32)],
        compiler_params=_cp(("parallel", "parallel")),
        name=f"matmul_qkv_rope_d{dil}",
    )(a, b, c, s1, s2)


def _rope_tab_body(pos_ref, invf_ref, c_ref, s1_ref, s2_ref):
    half = ROT_DIM // 2
    ang = pos_ref[...].astype(F32) * invf_ref[...]
    cos, sin = jnp.cos(ang), jnp.sin(ang)
    lane = lax.broadcasted_iota(I32, ang.shape, 1)
    c_ref[...] = jnp.where(lane < ROT_DIM, cos, 1.0)
    s1_ref[...] = jnp.where((lane >= half) & (lane < ROT_DIM), sin, 0.0)
    s2_ref[...] = jnp.where(lane < half, -sin, 0.0)


def rope_tables(positions, tm=512):
    S = positions.shape[0]
    half = ROT_DIM // 2
    inv_freq = ROPE_THETA ** (-jnp.arange(half, dtype=F32) / half)
    invf = jnp.zeros((1, HEAD_DIM), F32).at[0, :ROT_DIM].set(jnp.tile(inv_freq, 2))
    spec = pl.BlockSpec((tm, HEAD_DIM), lambda i: (i, 0))
    return pl.pallas_call(
        _rope_tab_body,
        grid=(S // tm,),
        in_specs=[pl.BlockSpec((tm, 1), lambda i: (i, 0)),
                  pl.BlockSpec((1, HEAD_DIM), lambda i: (0, 0))],
        out_specs=[spec, spec, spec],
        out_shape=[_sds((S, HEAD_DIM), F32)] * 3,
        compiler_params=_cp(("parallel",)),
        name="rope_tables",
    )(positions.reshape(S, 1), invf)


def _dwconv_ln_body(up_ref, um_ref, un_ref, w_ref, b_ref, g_ref, beta_ref, o_ref, pad, cv, shifted,
                    *, ts, lanes):
    i = pl.program_id(0)
    last = pl.num_programs(0) - 1
    H = CONV_HALO
    D = um_ref.shape[1]
    pad[0:H, :] = jnp.where(i > 0, up_ref[...], 0.0)
    pad[H:H + ts, :] = um_ref[...]
    pad[H + ts:H + ts + H, :] = jnp.where(i < last, un_ref[...], 0.0)
    off = H - CONV_WIDTH // 2
    for l0 in range(0, D, lanes):
        cols = slice(l0, l0 + lanes)
        out = None
        for b in range(8):
            vb = None
            for a in range(-(-(off + CONV_WIDTH) // 8)):
                k = 8 * a + b - off
                if 0 <= k < CONV_WIDTH:
                    term = pad[8 * a:8 * a + ts + 8, cols] * w_ref[k:k + 1, cols]
                    vb = term if vb is None else vb + term
            shifted[b] = vb
            sh = shifted[b, b:b + ts, :]
            out = sh if out is None else out + sh
        cv[:, cols] = out
    u = cv[...] + b_ref[...]
    mu = jnp.mean(u, axis=-1, keepdims=True)
    uc = u - mu
    y = uc * lax.rsqrt(jnp.mean(uc * uc, axis=-1, keepdims=True) + RMS_EPS)
    y = y * g_ref[...] + beta_ref[...]
    o_ref[...] = (y * jax.nn.sigmoid(y)).astype(o_ref.dtype)


def dwconv_ln_silu(u, w, b, g, beta, ts=128, lanes=512):
    S, D = u.shape
    H = CONV_HALO
    nh = ts // H
    vec = pl.BlockSpec((1, D), lambda i: (0, 0))
    return pl.pallas_call(
        functools.partial(_dwconv_ln_body, ts=ts, lanes=lanes),
        grid=(S // ts,),
        in_specs=[pl.BlockSpec((H, D), lambda i: (jnp.maximum(i * nh - 1, 0), 0)),
                  pl.BlockSpec((ts, D), lambda i: (i, 0)),
                  pl.BlockSpec((H, D), lambda i: (jnp.minimum((i + 1) * nh, S // H - 1), 0)),
                  pl.BlockSpec((CONV_WIDTH, D), lambda i: (0, 0)),
                  vec, vec, vec],
        out_specs=pl.BlockSpec((ts, D), lambda i: (i, 0)),
        out_shape=_sds((S, D), BF16),
        scratch_shapes=[pltpu.VMEM((ts + 2 * H, D), F32), pltpu.VMEM((ts, D), F32),
                        pltpu.VMEM((8, ts + 8, lanes), F32)],
        compiler_params=_cp(("parallel",)),
        name="dwconv_ln_silu",
    )(u, u, u, w, b.reshape(1, D), g.reshape(1, D), beta.reshape(1, D))


ATTN_SUB = 128


def _band_attn_body(q_ref, kp_ref, km_ref, kn_ref, vp_ref, vm_ref, vn_ref, o_ref, lse_ref, kc, vc,
                    *, tq, L):
    i = pl.program_id(1)
    Hh = ATTN_HALF
    kc[0:Hh, :] = kp_ref[...]
    kc[Hh:Hh + tq, :] = km_ref[...]
    kc[Hh + tq:Hh + tq + Hh, :] = kn_ref[...]
    vc[0:Hh, :] = vp_ref[...]
    vc[Hh:Hh + tq, :] = vm_ref[...]
    vc[Hh + tq:Hh + tq + Hh, :] = vn_ref[...]
    nk = ATTN_SUB + 2 * Hh
    t = lax.broadcasted_iota(I32, (ATTN_SUB, nk), 0)
    c = lax.broadcasted_iota(I32, (ATTN_SUB, nk), 1)
    lane_h = lax.broadcasted_iota(I32, (ATTN_SUB, LANES), 1)
    for s in range(tq // ATTN_SUB):
        row0 = s * ATTN_SUB
        kpos = i * tq + (row0 - Hh) + c
        mask = (c >= t) & (c <= t + 2 * Hh) & (kpos >= 0) & (kpos < L)
        lse_tile = jnp.zeros((ATTN_SUB, LANES), F32)
        for h in range(ATTN_HEADS):
            cols = slice(h * HEAD_DIM, (h + 1) * HEAD_DIM)
            q = q_ref[row0:row0 + ATTN_SUB, cols]
            k = kc[row0:row0 + nk, cols]
            v = vc[row0:row0 + nk, cols]
            logits = lax.dot_general(q, k, (((1,), (1,)), ((), ())), preferred_element_type=F32)
            logits = jnp.where(mask, logits, -jnp.inf)
            mx = jnp.max(logits, axis=-1, keepdims=True)
            p = jnp.exp(logits - mx)
            l = jnp.sum(p, axis=-1, keepdims=True)
            o = _dot(p.astype(BF16), v) / l
            o_ref[row0:row0 + ATTN_SUB, cols] = o.astype(o_ref.dtype)
            lse_tile = jnp.where(lane_h == h, mx + jnp.log(l), lse_tile)
        lse_ref[row0:row0 + ATTN_SUB, :] = lse_tile


def band_attention(qkv, dil, tq=256):
    _, L, C = qkv.shape
    S = dil * L
    tq = min(tq, L)
    x2 = qkv.reshape(S, C)
    nq = L // tq
    hb = tq // ATTN_HALF
    nhalo = S // ATTN_HALF

    def main(kind):
        return pl.BlockSpec((tq, HEADS_COLS), lambda r, i: (r * nq + i, kind))

    def prev(kind):
        return pl.BlockSpec((ATTN_HALF, HEADS_COLS),
                            lambda r, i: (jnp.maximum((r * nq + i) * hb - 1, 0), kind))

    def nxt(kind):
        return pl.BlockSpec((ATTN_HALF, HEADS_COLS),
                            lambda r, i: (jnp.minimum((r * nq + i + 1) * hb, nhalo - 1), kind))

    o, lse = pl.pallas_call(
        functools.partial(_band_attn_body, tq=tq, L=L),
        grid=(dil, nq),
        in_specs=[main(0), prev(1), main(1), nxt(1), prev(2), main(2), nxt(2)],
        out_specs=[pl.BlockSpec((tq, HEADS_COLS), lambda r, i: (r * nq + i, 0)),
                   pl.BlockSpec((tq, LANES), lambda r, i: (r * nq + i, 0))],
        out_shape=[_sds((S, HEADS_COLS), BF16), _sds((S, LANES), F32)],
        scratch_shapes=[pltpu.VMEM((tq + 2 * ATTN_HALF, HEADS_COLS), BF16),
                        pltpu.VMEM((tq + 2 * ATTN_HALF, HEADS_COLS), BF16)],
        compiler_params=_cp(("parallel", "parallel")),
        name=f"band_attention_d{dil}",
    )(x2, x2, x2, x2, x2, x2, x2)
    return o.reshape(dil, L, HEADS_COLS), lse.reshape(dil, L, LANES)


def _attn_combine_body(o0_ref, o1_ref, o2_ref, l0_ref, l1_ref, l2_ref, out_ref, o_nat, l_nat):
    for g, (o_ref, l_ref) in enumerate(((o1_ref, l1_ref), (o2_ref, l2_ref))):
        dil, rows = o_ref.shape[0], o_ref.shape[1]
        for r in range(dil):
            l_nat[g, pl.ds(r, rows, stride=dil), :] = l_ref[r]
            for h in range(ATTN_HEADS):
                o_nat[g, h, pl.ds(r, rows, stride=dil), :] = (
                    o_ref[r, :, h * HEAD_DIM:(h + 1) * HEAD_DIM].astype(F32))
    l0, l1, l2 = l0_ref[0], l_nat[0], l_nat[1]
    mx = jnp.maximum(jnp.maximum(l0, l1), l2)
    e0, e1, e2 = jnp.exp(l0 - mx), jnp.exp(l1 - mx), jnp.exp(l2 - mx)
    den = e0 + e1 + e2
    a0, a1, a2 = e0 / den, e1 / den, e2 / den
    for h in range(ATTN_HEADS):
        cols = slice(h * HEAD_DIM, (h + 1) * HEAD_DIM)
        acc = (a0[:, h:h + 1] * o0_ref[0, :, cols].astype(F32)
               + a1[:, h:h + 1] * o_nat[0, h]
               + a2[:, h:h + 1] * o_nat[1, h])
        out_ref[:, cols] = acc.astype(out_ref.dtype)


def attn_combine(outs, lses, tm=256):
    S = outs[0].shape[0] * outs[0].shape[1]

    def spec(a):
        dil = a.shape[0]
        return pl.BlockSpec((dil, tm // dil, a.shape[2]), lambda i: (0, i, 0))

    return pl.pallas_call(
        _attn_combine_body,
        grid=(S // tm,),
        in_specs=[spec(a) for a in (*outs, *lses)],
        out_specs=pl.BlockSpec((tm, HEADS_COLS), lambda i: (i, 0)),
        out_shape=_sds((S, HEADS_COLS), BF16),
        scratch_shapes=[pltpu.VMEM((2, ATTN_HEADS, tm, HEAD_DIM), F32), pltpu.VMEM((2, tm, LANES), F32)],
        compiler_params=_cp(("parallel",)),
        name="attn_combine",
    )(*outs, *lses)


def _row_to_col(row, eye):
    return jnp.sum(jnp.where(eye, row, 0.0), axis=1, keepdims=True)


def _chunk_scan(x, L, op, reverse):
    n = x.shape[1]
    pos = lax.broadcasted_iota(I32, x.shape, 1) % L
    d = 1
    while d < L:
        if reverse:
            x = jnp.where(pos < L - d, op(x, pltpu.roll(x, n - d, 1)), x)
        else:
            x = jnp.where(pos >= d, op(x, pltpu.roll(x, d, 1)), x)
        d *= 2
    return x


def _mlstm_gates_body(g_ref, rows_ref, cols_ref, *, L, H):
    x = GATE_SOFTCAP * jnp.tanh(g_ref[...] / GATE_SOFTCAP)
    logf = -(jnp.maximum(-x, 0.0) + jnp.log1p(jnp.exp(-jnp.abs(x))))
    li = jnp.concatenate([x[0:H], x[2 * H:3 * H]], axis=0)
    lf = jnp.concatenate([logf[H:2 * H], logf[3 * H:4 * H]], axis=0)
    bwd = lax.broadcasted_iota(I32, li.shape, 0) >= H
    pre = _chunk_scan(lf, L, jnp.add, False)
    suf = _chunk_scan(lf, L, jnp.add, True)
    total = pre + suf - lf
    b = jnp.where(bwd, suf, pre)
    g = li - b
    pmax = _chunk_scan(g, L, jnp.maximum, False)
    smax = _chunk_scan(g, L, jnp.maximum, True)
    mi = b + jnp.where(bwd, smax, pmax)
    a = total + g
    maxa = total + jnp.maximum(pmax, smax)
    rows_ref[...] = jnp.concatenate([g, a, total, maxa], axis=0)
    n = x.shape[1]
    m = jnp.concatenate([b, a, mi, jnp.zeros((LANES - 6 * H, n), F32)], axis=0)
    for c in range(n // LANES):
        cols_ref[c * LANES:(c + 1) * LANES, :] = m[:, c * LANES:(c + 1) * LANES].T


def mlstm_gates(gates, ts=2048):
    R, S = gates.shape
    H = R // 4
    ts = min(ts, S)
    return pl.pallas_call(
        functools.partial(_mlstm_gates_body, L=MLSTM_CHUNK, H=H),
        grid=(S // ts,),
        in_specs=[pl.BlockSpec((R, ts), lambda i: (0, i))],
        out_specs=[pl.BlockSpec((2 * R, ts), lambda i: (0, i)),
                   pl.BlockSpec((ts, LANES), lambda i: (i, 0))],
        out_shape=[_sds((2 * R, S), F32), _sds((S, LANES), F32)],
        compiler_params=_cp(("parallel",)),
        name="mlstm_gates",
    )(gates)


def _mlstm_chunk(q, k, v, gate, C_ref, n_ref, m_ref, rev, L):
    g_row, a_row, total, maxa, b_col, a_col, mi_col = gate
    q = q * (MLSTM_DQK ** -0.5)
    t_idx = lax.broadcasted_iota(I32, (L, L), 0)
    s_idx = lax.broadcasted_iota(I32, (L, L), 1)
    causal = s_idx >= t_idx if rev else s_idx <= t_idx

    m = m_ref[...]
    C = C_ref[...]
    n = n_ref[...]
    d_intra = jnp.where(causal, b_col + g_row, -jnp.inf)
    d_inter = b_col + m
    m_t = jnp.maximum(d_inter, mi_col)
    w_inter = jnp.exp(d_inter - m_t)
    s = lax.dot_general(q, k, (((1,), (1,)), ((), ())), preferred_element_type=F32) * jnp.exp(d_intra - m_t)
    num = w_inter * _dot(q, C.astype(BF16)) + _dot(s.astype(BF16), v)
    den = (w_inter * jnp.sum(q.astype(F32) * n, axis=1, keepdims=True)
           + jnp.sum(s, axis=1, keepdims=True))
    h = num / jnp.maximum(jnp.abs(den), jnp.exp(-m_t))

    m_new = jnp.maximum(total + m, maxa)
    decay = jnp.exp(total + m - m_new)
    kf = k.astype(F32)
    wkT = kf.T * jnp.exp(a_row - m_new)
    C_ref[...] = decay * C + _dot(wkT.astype(BF16), v)
    n_ref[...] = decay * n + jnp.sum(jnp.exp(a_col - m_new) * kf, axis=0, keepdims=True)
    m_ref[...] = m_new
    return h


def _mlstm_body(qf_ref, kf_ref, vf_ref, rf_ref, cf_ref, qb_ref, kb_ref, vb_ref, rb_ref, cb_ref,
                of_ref, ob_ref, C_ref, n_ref, m_ref, *, L, H):
    @pl.when(pl.program_id(0) == 0)
    def _():
        C_ref[...] = jnp.zeros_like(C_ref)
        n_ref[...] = jnp.zeros_like(n_ref)
        m_ref[...] = jnp.zeros_like(m_ref)

    for hh in range(H):
        qk = slice(hh * MLSTM_DQK, (hh + 1) * MLSTM_DQK)
        vv = slice(hh * MLSTM_DV, (hh + 1) * MLSTM_DV)
        for rev, (q_ref, k_ref, v_ref, r_ref, c_ref, o_ref) in enumerate(
                ((qf_ref, kf_ref, vf_ref, rf_ref, cf_ref, of_ref),
                 (qb_ref, kb_ref, vb_ref, rb_ref, cb_ref, ob_ref))):
            r = rev * H + hh
            nr = 2 * H
            gate = (r_ref[r], r_ref[nr + r], r_ref[2 * nr + r][:, 0:1], r_ref[3 * nr + r][:, 0:1],
                    c_ref[:, r:r + 1], c_ref[:, nr + r:nr + r + 1], c_ref[:, 2 * nr + r:2 * nr + r + 1])
            o_ref[:, vv] = _mlstm_chunk(q_ref[:, qk], k_ref[:, qk], v_ref[:, vv], gate,
                                        C_ref.at[r], n_ref.at[r], m_ref.at[r], bool(rev), L)


def mlstm_scan(proj, rows, cols):
    S = proj.shape[0]
    H, L = MLSTM_HEADS, MLSTM_CHUNK
    NC = S // L
    R = rows.shape[0]
    kv = (2 * H * MLSTM_DQK) // (H * MLSTM_DV)
    r3 = rows.reshape(R, 1, S)

    def specs(rev):
        def ch(c):
            return NC - 1 - c if rev else c
        return [pl.BlockSpec((L, H * MLSTM_DQK), lambda c: (ch(c), 0)),
                pl.BlockSpec((L, H * MLSTM_DQK), lambda c: (ch(c), 1)),
                pl.BlockSpec((L, H * MLSTM_DV), lambda c: (ch(c), kv)),
                pl.BlockSpec((R, 1, L), lambda c: (0, 0, ch(c))),
                pl.BlockSpec((L, LANES), lambda c: (ch(c), 0))]

    return pl.pallas_call(
        functools.partial(_mlstm_body, L=L, H=H),
        grid=(NC,),
        in_specs=specs(0) + specs(1),
        out_specs=[pl.BlockSpec((L, H * MLSTM_DV), lambda c: (c, 0)),
                   pl.BlockSpec((L, H * MLSTM_DV), lambda c: (NC - 1 - c, 0))],
        out_shape=[_sds((S, H * MLSTM_DV), F32), _sds((S, H * MLSTM_DV), F32)],
        scratch_shapes=[pltpu.VMEM((2 * H, MLSTM_DQK, MLSTM_DV), F32),
                        pltpu.VMEM((2 * H, 1, MLSTM_DQK), F32),
                        pltpu.VMEM((2 * H, 1, 1), F32)],
        compiler_params=_cp(("arbitrary",)),
        name="mlstm_scan",
    )(proj, proj, proj, r3, cols, proj, proj, proj, r3, cols)


def _mlstm_out_body(hf_ref, hb_ref, o_ref, hn_ref, y_ref):
    hs = hf_ref[...] + hb_ref[...]
    for h in range(MLSTM_HEADS):
        cols = slice(h * MLSTM_DV, (h + 1) * MLSTM_DV)
        x = hs[:, cols]
        x = x * lax.rsqrt(jnp.mean(x * x, axis=-1, keepdims=True) + RMS_EPS) * hn_ref[:, cols]
        y_ref[:, cols] = (x * jax.nn.sigmoid(o_ref[:, cols].astype(F32))).astype(y_ref.dtype)


def mlstm_out(hf, hb, proj, head_norm, tm=256):
    S, D = hf.shape
    ob = (2 * MLSTM_HEADS * MLSTM_DQK + D) // D
    return pl.pallas_call(
        _mlstm_out_body,
        grid=(S // tm,),
        in_specs=[pl.BlockSpec((tm, D), lambda i: (i, 0)),
                  pl.BlockSpec((tm, D), lambda i: (i, 0)),
                  pl.BlockSpec((tm, D), lambda i: (i, ob)),
                  pl.BlockSpec((1, D), lambda i: (0, 0))],
        out_specs=pl.BlockSpec((tm, D), lambda i: (i, 0)),
        out_shape=_sds((S, D), BF16),
        compiler_params=_cp(("parallel",)),
        name="mlstm_out",
    )(hf, hb, proj, head_norm.reshape(1, D))


def _excl_prefix(mask, U, Ls):
    W = _dot(mask.astype(BF16), U)
    totb = jnp.broadcast_to(W[:, LANES - 1:LANES], W.shape)
    offb = _dot(Ls, totb.astype(BF16))
    return offb + W - mask, W, offb


def _route_body(lg_ref, idx_ref, gate_ref, aff_s, gt_s, eq_s, need_s, *, NCH, cap):
    E = lg_ref.shape[0]
    lg = lg_ref[...]
    mx = jnp.max(lg, axis=0, keepdims=True)
    ex = jnp.exp(lg - mx)
    aff = ex / jnp.sum(ex, axis=0, keepdims=True)
    bits = lax.bitcast_convert_type(aff, I32)

    def count(m):
        return jnp.sum(jnp.sum(m.astype(F32), axis=1, keepdims=True), axis=2, keepdims=True)

    def search(it, T):
        cand = T | lax.shift_left(jnp.int32(1), 30 - it)
        return jnp.where(count(bits >= cand) >= cap, cand, T)

    T = lax.fori_loop(0, 31, search, jnp.zeros((E, 1, 1), I32))
    gt = bits > T
    aff_s[...] = aff
    gt_s[...] = gt.astype(F32)
    eq_s[...] = (bits == T).astype(F32)
    need_s[...] = jnp.broadcast_to(cap - count(gt), need_s.shape)

    r = lax.broadcasted_iota(I32, (LANES, LANES), 0)
    cl = lax.broadcasted_iota(I32, (LANES, LANES), 1)
    U = (r <= cl).astype(BF16)
    r2 = lax.broadcasted_iota(I32, (NCH, NCH), 0)
    c2 = lax.broadcasted_iota(I32, (NCH, NCH), 1)
    Ls = (c2 < r2).astype(BF16)
    j_row = lax.broadcasted_iota(I32, (1, cap), 1).astype(F32)
    c_iota = lax.broadcasted_iota(I32, (NCH, cap), 0).astype(F32)
    l_iota = lax.broadcasted_iota(I32, (LANES, cap), 0).astype(F32)

    def per_expert(e, carry):
        eq = eq_s[e]
        need = need_s[e][0:1, 0:1]
        rank_eq, _, _ = _excl_prefix(eq, U, Ls)
        sel = gt_s[e] + eq * (rank_eq < need).astype(F32)
        _, W, offb = _excl_prefix(sel, U, Ls)
        off_col = offb[:, 0:1]
        incl_col = off_col + W[:, LANES - 1:LANES]
        cidx = jnp.sum((incl_col <= j_row).astype(F32), axis=0, keepdims=True)
        onehot = (c_iota == cidx).astype(F32)
        local = j_row - jnp.sum(off_col * onehot, axis=0, keepdims=True)
        oh = onehot.astype(BF16)
        Wsel = _dot(W.T.astype(BF16), oh)
        lidx = jnp.sum((Wsel <= local).astype(F32), axis=0, keepdims=True)
        idx_ref[e] = (cidx * LANES + lidx).astype(I32)
        at = aff_s[e].T
        hi = at.astype(BF16)
        r1 = at - hi.astype(F32)
        mid = r1.astype(BF16)
        lo = (r1 - mid.astype(F32)).astype(BF16)
        asel = _dot(hi, oh) + _dot(mid, oh) + _dot(lo, oh)
        gate_ref[e] = jnp.sum(jnp.where(l_iota == lidx, asel, 0.0), axis=0, keepdims=True)
        return carry

    lax.fori_loop(0, E, per_expert, 0)


def route(logits_t, cap):
    E, S = logits_t.shape
    NCH = S // LANES
    full = pl.BlockSpec((E, NCH, LANES), lambda: (0, 0, 0))
    ospec = pl.BlockSpec((E, 1, cap), lambda: (0, 0, 0))
    return pl.pallas_call(
        functools.partial(_route_body, NCH=NCH, cap=cap),
        in_specs=[full],
        out_specs=[ospec, ospec],
        out_shape=[_sds((E, 1, cap), I32), _sds((E, 1, cap), F32)],
        scratch_shapes=[pltpu.VMEM((E, NCH, LANES), F32), pltpu.VMEM((E, NCH, LANES), F32),
                        pltpu.VMEM((E, NCH, LANES), F32), pltpu.VMEM((E, 8, LANES), F32)],
        compiler_params=pltpu.CompilerParams(vmem_limit_bytes=VMEM_LIMIT),
        name="route",
    )(logits_t.reshape(E, NCH, LANES))


SUBLANES = 8


def _tiled_rows(x):
    return x.reshape(x.shape[0] // SUBLANES, SUBLANES, x.shape[1])


def _issue_row_copies(idx_ref, tm, make_copy):
    def body(g, carry):
        for k in range(SUBLANES):
            row = idx_ref[0, 0, g * SUBLANES + k]
            make_copy(g, k, lax.shift_right_logical(row, 3), row & (SUBLANES - 1)).start()
        return carry
    lax.fori_loop(0, tm // SUBLANES, body, 0)


def _issue_row_gather(idx_ref, src_hbm, dst, sem, tm):
    _issue_row_copies(idx_ref, tm, lambda g, k, tile, sub: pltpu.make_async_copy(
        src_hbm.at[tile, pl.ds(sub, 1)], dst.at[g, pl.ds(k, 1)], sem))


def _moe_up_body(idx_ref, idxn_ref, hn_hbm, wg_ref, wu_ref, o_ref, buf, sem, *, tm):
    nt = pl.num_programs(1)
    step = pl.program_id(0) * nt + pl.program_id(1)
    nsteps = pl.num_programs(0) * nt
    slot = step % 2

    @pl.when(step == 0)
    def _():
        _issue_row_gather(idx_ref, hn_hbm, buf.at[0], sem.at[0], tm)

    @pl.when(step + 1 < nsteps)
    def _():
        _issue_row_gather(idxn_ref, hn_hbm, buf.at[1 - slot], sem.at[1 - slot], tm)

    pltpu.make_async_copy(hn_hbm.at[pl.ds(0, tm // SUBLANES)], buf.at[slot], sem.at[slot]).wait()
    xb = buf[slot].reshape(tm, buf.shape[-1]).astype(BF16)
    g = _dot(xb, wg_ref[...])
    u = _dot(xb, wu_ref[...])
    o_ref[...] = (g * jax.nn.sigmoid(g) * u).astype(o_ref.dtype)


def moe_up(hn, idx, wg, wu, layer, tm):
    E, _, cap = idx.shape
    D = hn.shape[1]
    F = wg.shape[3]
    nt = cap // tm
    nsteps = E * nt

    def nxt(e, t):
        s = jnp.minimum(e * nt + t + 1, nsteps - 1)
        return (s // nt, 0, s % nt)

    wspec = pl.BlockSpec((None, None, D, F), lambda e, t: (layer, e, 0, 0))
    return pl.pallas_call(
        functools.partial(_moe_up_body, tm=tm),
        grid=(E, nt),
        in_specs=[pl.BlockSpec((1, 1, tm), lambda e, t: (e, 0, t), memory_space=pltpu.SMEM),
                  pl.BlockSpec((1, 1, tm), nxt, memory_space=pltpu.SMEM),
                  pl.BlockSpec(memory_space=pl.ANY),
                  wspec, wspec],
        out_specs=pl.BlockSpec((tm, F), lambda e, t: (e * nt + t, 0)),
        out_shape=_sds((E * cap, F), BF16),
        scratch_shapes=[pltpu.VMEM((2, tm // SUBLANES, SUBLANES, D), F32), pltpu.SemaphoreType.DMA((2,))],
        compiler_params=_cp(("arbitrary", "arbitrary")),
        name="moe_up",
    )(idx, idx, _tiled_rows(hn), wg, wu)


DOWN_SLOTS = 3


def _moe_down_body(idx_ref, idxn_ref, gate_ref, h_ref, wd_ref, x_hbm, xo_hbm, xbuf, gsem, ssem, *, tm):
    del x_hbm
    nt = pl.num_programs(1)
    t = pl.program_id(1)
    step = pl.program_id(0) * nt + t
    nsteps = pl.num_programs(0) * nt
    cur = step % DOWN_SLOTS
    nxt = (step + 1) % DOWN_SLOTS
    prv = (step + 2) % DOWN_SLOTS

    def wait_scatter(slot):
        pltpu.make_async_copy(xbuf.at[slot], xo_hbm.at[pl.ds(0, tm // SUBLANES)], ssem.at[slot]).wait()

    @pl.when((step >= 2) & (t != 1))
    def _():
        wait_scatter(nxt)

    @pl.when((t == 0) & (step >= 1))
    def _():
        wait_scatter(prv)

    @pl.when(t == 0)
    def _():
        _issue_row_gather(idx_ref, xo_hbm, xbuf.at[cur], gsem.at[cur], tm)

    @pl.when(t + 1 < nt)
    def _():
        _issue_row_gather(idxn_ref, xo_hbm, xbuf.at[nxt], gsem.at[nxt], tm)

    y = _dot(h_ref[...], wd_ref[...])
    t_idx = lax.broadcasted_iota(I32, (tm, tm), 0)
    s_idx = lax.broadcasted_iota(I32, (tm, tm), 1)
    gate_col = _row_to_col(gate_ref[0], t_idx == s_idx)
    pltpu.make_async_copy(xo_hbm.at[pl.ds(0, tm // SUBLANES)], xbuf.at[cur], gsem.at[cur]).wait()
    upd = xbuf[cur].reshape(y.shape) + y * gate_col
    xbuf[cur] = upd.reshape(xbuf.shape[1:])
    _issue_row_copies(idx_ref, tm, lambda g, k, tile, sub: pltpu.make_async_copy(
        xbuf.at[cur, g, pl.ds(k, 1)], xo_hbm.at[tile, pl.ds(sub, 1)], ssem.at[cur]))

    @pl.when(step == nsteps - 1)
    def _():
        wait_scatter(prv)
        wait_scatter(cur)


def moe_down(x, hmid, idx, gate, wd, layer, tm):
    E, _, cap = idx.shape
    S, D = x.shape
    F = wd.shape[2]
    nt = cap // tm
    assert nt >= 2, "the scatter wait schedule needs at least two tiles per expert"
    nsteps = E * nt

    def nxt(e, t):
        s = jnp.minimum(e * nt + t + 1, nsteps - 1)
        return (s // nt, 0, s % nt)

    return pl.pallas_call(
        functools.partial(_moe_down_body, tm=tm),
        grid=(E, nt),
        in_specs=[pl.BlockSpec((1, 1, tm), lambda e, t: (e, 0, t), memory_space=pltpu.SMEM),
                  pl.BlockSpec((1, 1, tm), nxt, memory_space=pltpu.SMEM),
                  pl.BlockSpec((None, 1, tm), lambda e, t: (e, 0, t)),
                  pl.BlockSpec((tm, F), lambda e, t: (e * nt + t, 0)),
                  pl.BlockSpec((None, None, F, D), lambda e, t: (layer, e, 0, 0)),
                  pl.BlockSpec(memory_space=pl.ANY)],
        out_specs=pl.BlockSpec(memory_space=pl.ANY),
        out_shape=_sds((S // SUBLANES, SUBLANES, D), F32),
        input_output_aliases={5: 0},
        scratch_shapes=[pltpu.VMEM((DOWN_SLOTS, tm // SUBLANES, SUBLANES, D), F32),
                        pltpu.SemaphoreType.DMA((DOWN_SLOTS,)),
                        pltpu.SemaphoreType.DMA((DOWN_SLOTS,))],
        compiler_params=_cp(("arbitrary", "arbitrary")),
        name="moe_down",
    )(idx, idx, gate, hmid, wd, _tiled_rows(x)).reshape(S, D)


def expert_choice_ffn(x, norm_g, w_router, w_gate, w_up, w_down, layer):
    S = x.shape[0]
    E = w_router.shape[1]
    cap = CAPACITY_FACTOR * S // E
    hn, logits_t = rmsnorm_proj(x, norm_g, w_router, jnp.zeros((E,), F32), F32)
    idx, gate = route(logits_t, cap)
    tm = min(MOE_TILE, cap // 2)
    hmid = moe_up(hn, idx, w_gate, w_up, layer, tm)
    return moe_down(x, hmid, idx, gate, w_down, layer, tm)


def conformer_mixer(x, norm_g, w_in, b_in, w_dw, b_dw, ln_g, ln_b, w_out, b_out, j):
    h = rmsnorm(x, norm_g, BF16)
    u = matmul_glu(h, w_in, j, b_in)
    u = dwconv_ln_silu(u, w_dw, b_dw, ln_g, ln_b)
    return matmul_bias_residual(u, w_out, j, b_out, x)


def attention_mixer(x, norm_g, positions, w_qkv, w_out, j):
    h = rmsnorm(x, norm_g, BF16)
    tabs = rope_tables(positions)
    outs, lses = [], []
    for g, (window, dil) in enumerate(DIL_PATTERNS):
        assert window // (2 * dil) == ATTN_HALF
        o, lse = band_attention(matmul_qkv_rope(h, w_qkv, tabs, g, dil), dil)
        outs.append(o)
        lses.append(lse)
    o = attn_combine(outs, lses)
    return matmul_bias_residual(o, w_out, j, jnp.zeros((w_out.shape[2],), F32), x)


def mlstm_mixer(x, norm_g, w_in, b_gates, head_norm, w_out, j):
    D = x.shape[1]
    n_main = 2 * MLSTM_HEADS * MLSTM_DQK + 2 * D
    h, gates = rmsnorm_proj(x, norm_g, w_in[:, n_main:], b_gates, BF16)
    proj = matmul_plain(h, w_in[:, :n_main].astype(BF16), BF16)
    hf, hb = mlstm_scan(proj, *mlstm_gates(gates))
    y = mlstm_out(hf, hb, proj, head_norm)
    return matmul_bias_residual(y, w_out, j, jnp.zeros((D,), F32), x)


def kernel(x, positions, mix_norm, ffn_norm, conv_w_in, conv_b_in, conv_w_dw, conv_b_dw, conv_ln_g,
           conv_ln_b, conv_w_out, conv_b_out, attn_w_qkv, attn_w_out, mlstm_w_in, mlstm_b_gates,
           mlstm_head_norm, mlstm_w_out, router_w, moe_w_gate, moe_w_up, moe_w_down, final_norm):
    assert x.shape[0] == 1, "single-sequence batch"
    xs = x[0]
    pos = positions[0]
    depth = mix_norm.shape[0]
    conv_w_in, conv_w_out = conv_w_in.astype(BF16), conv_w_out.astype(BF16)
    attn_w_out, mlstm_w_out = attn_w_out.astype(BF16), mlstm_w_out.astype(BF16)
    moe_w_gate, moe_w_up, moe_w_down = (w.astype(BF16) for w in (moe_w_gate, moe_w_up, moe_w_down))
    for i in range(depth):
        kind, j = i % 3, i // 3
        if kind == 0:
            xs = conformer_mixer(xs, mix_norm[i], conv_w_in, conv_b_in[j], conv_w_dw[j], conv_b_dw[j],
                                 conv_ln_g[j], conv_ln_b[j], conv_w_out, conv_b_out[j], j)
        elif kind == 1:
            xs = attention_mixer(xs, mix_norm[i], pos, attn_w_qkv[j].astype(BF16), attn_w_out, j)
        else:
            xs = mlstm_mixer(xs, mix_norm[i], mlstm_w_in[j], mlstm_b_gates[j], mlstm_head_norm[j],
                             mlstm_w_out, j)
        xs = expert_choice_ffn(xs, ffn_norm[i], router_w[i], moe_w_gate, moe_w_up, moe_w_down, i)
    return rmsnorm(xs, final_norm, F32)[None]
```

```python
import functools

import jax
import jax.numpy as jnp
from jax import lax
from jax.experimental import pallas as pl
from jax.experimental.pallas import tpu as pltpu

F32 = jnp.float32
BF16 = jnp.bfloat16
I32 = jnp.int32

RMS_EPS = 1e-6
CONV_WIDTH = 31
CONV_HALO = 16
HEAD_DIM = 128
ATTN_HEADS = 16
DIL_PATTERNS = ((128, 1), (512, 4), (2048, 16))
ATTN_HALF = 64
ROT_DIM = 32
ROPE_THETA = 500000.0
GROUP_COLS = 3 * ATTN_HEADS * HEAD_DIM
HEADS_COLS = ATTN_HEADS * HEAD_DIM
MLSTM_HEADS = 8
MLSTM_DQK = 256
MLSTM_DV = 512
MLSTM_CHUNK = 128
GATE_SOFTCAP = 15.0
N_EXPERTS = 16
MOE_TILE = 512
MOE_K_CHUNKS = 8
CAPACITY_FACTOR = 2
LANES = 128
MXU_COLS = 256
VMEM_LIMIT = 56 * 1024 * 1024


def _cp(sem, vmem=VMEM_LIMIT):
    return pltpu.CompilerParams(dimension_semantics=sem, vmem_limit_bytes=vmem)


def _sds(shape, dtype):
    return jax.ShapeDtypeStruct(shape, dtype)


def _rmsnorm_body(x_ref, g_ref, o_ref):
    x = x_ref[...]
    ms = jnp.mean(x * x, axis=-1, keepdims=True)
    o_ref[...] = (x * lax.rsqrt(ms + RMS_EPS) * g_ref[...]).astype(o_ref.dtype)


def rmsnorm(x, g, out_dtype, tm=256):
    S, D = x.shape
    return pl.pallas_call(
        _rmsnorm_body,
        grid=(S // tm,),
        in_specs=[pl.BlockSpec((tm, D), lambda i: (i, 0)),
                  pl.BlockSpec((1, D), lambda i: (0, 0))],
        out_specs=pl.BlockSpec((tm, D), lambda i: (i, 0)),
        out_shape=_sds((S, D), out_dtype),
        compiler_params=_cp(("parallel",)),
        name="rmsnorm",
    )(x, g.reshape(1, D))


def _rmsnorm_proj_body(x_ref, g_ref, wt_ref, b_ref, o_ref, p_ref, *, pack):
    x = x_ref[...]
    ms = jnp.mean(x * x, axis=-1, keepdims=True)
    y = x * lax.rsqrt(ms + RMS_EPS) * g_ref[...]
    if pack:
        half = y.shape[1] // 2
        bits = lax.bitcast_convert_type(y.astype(BF16).astype(F32), jnp.uint32)
        o_ref[...] = ((bits[:, half:] & jnp.uint32(0xFFFF0000))
                      | lax.shift_right_logical(bits[:, :half], jnp.uint32(16)))
    else:
        o_ref[...] = y.astype(o_ref.dtype)
    p = lax.dot_general(wt_ref[...], y, (((1,), (1,)), ((), ())),
                        precision=lax.Precision.HIGHEST, preferred_element_type=F32)
    p_ref[...] = p + b_ref[...]


def _unpack_bf16_pair(p, high):
    bits = p & jnp.uint32(0xFFFF0000) if high else lax.shift_left(p, jnp.uint32(16))
    return lax.bitcast_convert_type(bits, F32).astype(BF16)


def rmsnorm_proj(x, g, w, b, out_dtype, pack=False, tm=256):
    S, D = x.shape
    N = w.shape[1]
    Do, out_dtype = (D // 2, jnp.uint32) if pack else (D, out_dtype)
    return pl.pallas_call(
        functools.partial(_rmsnorm_proj_body, pack=pack),
        grid=(S // tm,),
        in_specs=[pl.BlockSpec((tm, D), lambda i: (i, 0)),
                  pl.BlockSpec((1, D), lambda i: (0, 0)),
                  pl.BlockSpec((N, D), lambda i: (0, 0)),
                  pl.BlockSpec((N, 1), lambda i: (0, 0))],
        out_specs=[pl.BlockSpec((tm, Do), lambda i: (i, 0)),
                   pl.BlockSpec((N, tm), lambda i: (0, i))],
        out_shape=[_sds((S, Do), out_dtype), _sds((N, S), F32)],
        compiler_params=_cp(("parallel",)),
        name="rmsnorm_proj",
    )(x, g.reshape(1, D), w.T, b.reshape(N, 1))


def _dot(a, b):
    return jnp.dot(a, b, preferred_element_type=F32)


def _mm_plain_body(a_ref, b_ref, o_ref):
    o_ref[...] = _dot(a_ref[...], b_ref[...]).astype(o_ref.dtype)


def matmul_plain(a, b, out_dtype, tm=512, tn=1024):
    M, K = a.shape
    N = b.shape[1]
    return pl.pallas_call(
        _mm_plain_body,
        grid=(N // tn, M // tm),
        in_specs=[pl.BlockSpec((tm, K), lambda j, i: (i, 0)),
                  pl.BlockSpec((K, tn), lambda j, i: (0, j))],
        out_specs=pl.BlockSpec((tm, tn), lambda j, i: (i, j)),
        out_shape=_sds((M, N), out_dtype),
        compiler_params=_cp(("parallel", "parallel")),
        name="matmul_plain",
    )(a, b)


def _mm_res_body(a_ref, b_ref, bias_ref, r_ref, o_ref):
    o_ref[...] = _dot(a_ref[...], b_ref[...]) + bias_ref[...] + r_ref[...]


def matmul_bias_residual(a, b, layer, bias, res, tm=512, tn=1024):
    M, K = a.shape
    N = b.shape[2]
    return pl.pallas_call(
        _mm_res_body,
        grid=(N // tn, M // tm),
        in_specs=[pl.BlockSpec((tm, K), lambda j, i: (i, 0)),
                  pl.BlockSpec((None, K, tn), lambda j, i: (layer, 0, j)),
                  pl.BlockSpec((1, tn), lambda j, i: (0, j)),
                  pl.BlockSpec((tm, tn), lambda j, i: (i, j))],
        out_specs=pl.BlockSpec((tm, tn), lambda j, i: (i, j)),
        out_shape=_sds((M, N), F32),
        compiler_params=_cp(("parallel", "parallel")),
        name="matmul_bias_residual",
    )(a, b, bias.reshape(1, N), res)


def _mm_glu_body(a_ref, b1_ref, b2_ref, bias1_ref, bias2_ref, o_ref):
    a = a_ref[...]
    u1 = _dot(a, b1_ref[...]) + bias1_ref[...]
    u2 = _dot(a, b2_ref[...]) + bias2_ref[...]
    o_ref[...] = u1 * jax.nn.sigmoid(u2)


def matmul_glu(a, b, layer, bias, tm=512, tn=1024):
    M, K = a.shape
    N2 = b.shape[2] // 2
    nb = N2 // tn
    bias = bias.reshape(1, 2 * N2)
    return pl.pallas_call(
        _mm_glu_body,
        grid=(nb, M // tm),
        in_specs=[pl.BlockSpec((tm, K), lambda j, i: (i, 0)),
                  pl.BlockSpec((None, K, tn), lambda j, i: (layer, 0, j)),
                  pl.BlockSpec((None, K, tn), lambda j, i: (layer, 0, j + nb)),
                  pl.BlockSpec((1, tn), lambda j, i: (0, j)),
                  pl.BlockSpec((1, tn), lambda j, i: (0, j + nb))],
        out_specs=pl.BlockSpec((tm, tn), lambda j, i: (i, j)),
        out_shape=_sds((M, N2), F32),
        compiler_params=_cp(("parallel", "parallel")),
        name="matmul_glu",
    )(a, b, b, bias, bias)


def _mm_rope_body(a_ref, b_ref, c_ref, s1_ref, s2_ref, o_ref, stage, *, tn, tiles_per_kind, dil):
    kind = pl.program_id(0) // tiles_per_kind
    rot = kind < 2
    scale = jnp.where(kind == 0, HEAD_DIM ** -0.5, 1.0).astype(F32)
    c = jnp.where(rot, c_ref[...], 1.0) * scale
    s1 = jnp.where(rot, s1_ref[...], 0.0) * scale
    s2 = jnp.where(rot, s2_ref[...], 0.0) * scale
    a = a_ref[...]
    rows = a.shape[0] // dil
    for ch in range(tn // MXU_COLS):
        acc = _dot(a, b_ref[:, ch * MXU_COLS:(ch + 1) * MXU_COLS])
        for hh in range(MXU_COLS // HEAD_DIM):
            hb = ch * (MXU_COLS // HEAD_DIM) + hh
            cols = slice(hb * HEAD_DIM, (hb + 1) * HEAD_DIM)
            t = acc[:, hh * HEAD_DIM:(hh + 1) * HEAD_DIM]
            val = (t * c + pltpu.roll(t, ROT_DIM // 2, 1) * s1
                   + pltpu.roll(t, HEAD_DIM - ROT_DIM // 2, 1) * s2)
            if dil == 1:
                o_ref[0, :, cols] = val.astype(o_ref.dtype)
            else:
                stage[hb] = val
                for r in range(dil):
                    o_ref[r, :, cols] = stage[hb, pl.ds(r, rows, stride=dil), :].astype(o_ref.dtype)


def matmul_qkv_rope(a, b, tabs, group, dil, tm=512, tn=2048):
    M, K = a.shape
    c, s1, s2 = tabs
    nj = GROUP_COLS // tn
    tab_spec = pl.BlockSpec((tm, HEAD_DIM), lambda j, i: (i, 0))
    return pl.pallas_call(
        functools.partial(_mm_rope_body, tn=tn, tiles_per_kind=HEADS_COLS // tn, dil=dil),
        grid=(nj, M // tm),
        in_specs=[pl.BlockSpec((tm, K), lambda j, i: (i, 0)),
                  pl.BlockSpec((K, tn), lambda j, i: (0, group * nj + j)),
                  tab_spec, tab_spec, tab_spec],
        out_specs=pl.BlockSpec((dil, tm // dil, tn), lambda j, i: (0, i, j)),
        out_shape=_sds((dil, M // dil, GROUP_COLS), BF16),
        scratch_shapes=[pltpu.VMEM((tn // HEAD_DIM, tm, HEAD_DIM) if dil > 1 else (1, 8, LANES), F32)],
        compiler_params=_cp(("parallel", "parallel")),
        name=f"matmul_qkv_rope_d{dil}",
    )(a, b, c, s1, s2)


def _rope_tab_body(pos_ref, invf_ref, c_ref, s1_ref, s2_ref):
    half = ROT_DIM // 2
    ang = pos_ref[...].astype(F32) * invf_ref[...]
    cos, sin = jnp.cos(ang), jnp.sin(ang)
    lane = lax.broadcasted_iota(I32, ang.shape, 1)
    c_ref[...] = jnp.where(lane < ROT_DIM, cos, 1.0)
    s1_ref[...] = jnp.where((lane >= half) & (lane < ROT_DIM), sin, 0.0)
    s2_ref[...] = jnp.where(lane < half, -sin, 0.0)


def rope_tables(positions, tm=512):
    S = positions.shape[0]
    half = ROT_DIM // 2
    inv_freq = ROPE_THETA ** (-jnp.arange(half, dtype=F32) / half)
    invf = jnp.zeros((1, HEAD_DIM), F32).at[0, :ROT_DIM].set(jnp.tile(inv_freq, 2))
    spec = pl.BlockSpec((tm, HEAD_DIM), lambda i: (i, 0))
    return pl.pallas_call(
        _rope_tab_body,
        grid=(S // tm,),
        in_specs=[pl.BlockSpec((tm, 1), lambda i: (i, 0)),
                  pl.BlockSpec((1, HEAD_DIM), lambda i: (0, 0))],
        out_specs=[spec, spec, spec],
        out_shape=[_sds((S, HEAD_DIM), F32)] * 3,
        compiler_params=_cp(("parallel",)),
        name="rope_tables",
    )(positions.reshape(S, 1), invf)


def _dwconv_ln_body(up_ref, um_ref, un_ref, w_ref, b_ref, g_ref, beta_ref, o_ref, pad, cv, shifted,
                    *, ts, lanes):
    i = pl.program_id(0)
    last = pl.num_programs(0) - 1
    H = CONV_HALO
    D = um_ref.shape[1]
    pad[0:H, :] = jnp.where(i > 0, up_ref[...], 0.0)
    pad[H:H + ts, :] = um_ref[...]
    pad[H + ts:H + ts + H, :] = jnp.where(i < last, un_ref[...], 0.0)
    off = H - CONV_WIDTH // 2
    for l0 in range(0, D, lanes):
        cols = slice(l0, l0 + lanes)
        out = None
        for b in range(8):
            vb = None
            for a in range(-(-(off + CONV_WIDTH) // 8)):
                k = 8 * a + b - off
                if 0 <= k < CONV_WIDTH:
                    term = pad[8 * a:8 * a + ts + 8, cols] * w_ref[k:k + 1, cols]
                    vb = term if vb is None else vb + term
            shifted[b] = vb
            sh = shifted[b, b:b + ts, :]
            out = sh if out is None else out + sh
        cv[:, cols] = out
    u = cv[...] + b_ref[...]
    mu = jnp.mean(u, axis=-1, keepdims=True)
    uc = u - mu
    y = uc * lax.rsqrt(jnp.mean(uc * uc, axis=-1, keepdims=True) + RMS_EPS)
    y = y * g_ref[...] + beta_ref[...]
    o_ref[...] = (y * jax.nn.sigmoid(y)).astype(o_ref.dtype)


def dwconv_ln_silu(u, w, b, g, beta, ts=128, lanes=512):
    S, D = u.shape
    H = CONV_HALO
    nh = ts // H
    vec = pl.BlockSpec((1, D), lambda i: (0, 0))
    return pl.pallas_call(
        functools.partial(_dwconv_ln_body, ts=ts, lanes=lanes),
        grid=(S // ts,),
        in_specs=[pl.BlockSpec((H, D), lambda i: (jnp.maximum(i * nh - 1, 0), 0)),
                  pl.BlockSpec((ts, D), lambda i: (i, 0)),
                  pl.BlockSpec((H, D), lambda i: (jnp.minimum((i + 1) * nh, S // H - 1), 0)),
                  pl.BlockSpec((CONV_WIDTH, D), lambda i: (0, 0)),
                  vec, vec, vec],
        out_specs=pl.BlockSpec((ts, D), lambda i: (i, 0)),
        out_shape=_sds((S, D), BF16),
        scratch_shapes=[pltpu.VMEM((ts + 2 * H, D), F32), pltpu.VMEM((ts, D), F32),
                        pltpu.VMEM((8, ts + 8, lanes), F32)],
        compiler_params=_cp(("parallel",)),
        name="dwconv_ln_silu",
    )(u, u, u, w, b.reshape(1, D), g.reshape(1, D), beta.reshape(1, D))


ATTN_SUB = 128


def _band_attn_body(q_ref, kp_ref, km_ref, kn_ref, vp_ref, vm_ref, vn_ref, o_ref, lse_ref, kc, vc,
                    *, tq, L):
    i = pl.program_id(1)
    Hh = ATTN_HALF
    kc[0:Hh, :] = kp_ref[...]
    kc[Hh:Hh + tq, :] = km_ref[...]
    kc[Hh + tq:Hh + tq + Hh, :] = kn_ref[...]
    vc[0:Hh, :] = vp_ref[...]
    vc[Hh:Hh + tq, :] = vm_ref[...]
    vc[Hh + tq:Hh + tq + Hh, :] = vn_ref[...]
    nk = ATTN_SUB + 2 * Hh
    t = lax.broadcasted_iota(I32, (ATTN_SUB, nk), 0)
    c = lax.broadcasted_iota(I32, (ATTN_SUB, nk), 1)
    lane_h = lax.broadcasted_iota(I32, (ATTN_SUB, LANES), 1)
    for s in range(tq // ATTN_SUB):
        row0 = s * ATTN_SUB
        kpos = i * tq + (row0 - Hh) + c
        mask = (c >= t) & (c <= t + 2 * Hh) & (kpos >= 0) & (kpos < L)
        lse_tile = jnp.zeros((ATTN_SUB, LANES), F32)
        for h in range(ATTN_HEADS):
            cols = slice(h * HEAD_DIM, (h + 1) * HEAD_DIM)
            q = q_ref[row0:row0 + ATTN_SUB, cols]
            k = kc[row0:row0 + nk, cols]
            v = vc[row0:row0 + nk, cols]
            logits = lax.dot_general(q, k, (((1,), (1,)), ((), ())), preferred_element_type=F32)
            logits = jnp.where(mask, logits, -jnp.inf)
            mx = jnp.max(logits, axis=-1, keepdims=True)
            p = jnp.exp(logits - mx)
            l = jnp.sum(p, axis=-1, keepdims=True)
            o = _dot(p.astype(BF16), v) / l
            o_ref[row0:row0 + ATTN_SUB, cols] = o.astype(o_ref.dtype)
            lse_tile = jnp.where(lane_h == h, mx + jnp.log(l), lse_tile)
        lse_ref[row0:row0 + ATTN_SUB, :] = lse_tile


def band_attention(qkv, dil, tq=256):
    _, L, C = qkv.shape
    S = dil * L
    tq = min(tq, L)
    x2 = qkv.reshape(S, C)
    nq = L // tq
    hb = tq // ATTN_HALF
    nhalo = S // ATTN_HALF

    def main(kind):
        return pl.BlockSpec((tq, HEADS_COLS), lambda r, i: (r * nq + i, kind))

    def prev(kind):
        return pl.BlockSpec((ATTN_HALF, HEADS_COLS),
                            lambda r, i: (jnp.maximum((r * nq + i) * hb - 1, 0), kind))

    def nxt(kind):
        return pl.BlockSpec((ATTN_HALF, HEADS_COLS),
                            lambda r, i: (jnp.minimum((r * nq + i + 1) * hb, nhalo - 1), kind))

    o, lse = pl.pallas_call(
        functools.partial(_band_attn_body, tq=tq, L=L),
        grid=(dil, nq),
        in_specs=[main(0), prev(1), main(1), nxt(1), prev(2), main(2), nxt(2)],
        out_specs=[pl.BlockSpec((tq, HEADS_COLS), lambda r, i: (r * nq + i, 0)),
                   pl.BlockSpec((tq, LANES), lambda r, i: (r * nq + i, 0))],
        out_shape=[_sds((S, HEADS_COLS), BF16), _sds((S, LANES), F32)],
        scratch_shapes=[pltpu.VMEM((tq + 2 * ATTN_HALF, HEADS_COLS), BF16),
                        pltpu.VMEM((tq + 2 * ATTN_HALF, HEADS_COLS), BF16)],
        compiler_params=_cp(("parallel", "parallel")),
        name=f"band_attention_d{dil}",
    )(x2, x2, x2, x2, x2, x2, x2)
    return o.reshape(dil, L, HEADS_COLS), lse.reshape(dil, L, LANES)


def _attn_combine_body(o0_ref, o1_ref, o2_ref, l0_ref, l1_ref, l2_ref, out_ref, o_nat, l_nat):
    for g, (o_ref, l_ref) in enumerate(((o1_ref, l1_ref), (o2_ref, l2_ref))):
        dil, rows = o_ref.shape[0], o_ref.shape[1]
        for r in range(dil):
            l_nat[g, pl.ds(r, rows, stride=dil), :] = l_ref[r]
            for h in range(ATTN_HEADS):
                o_nat[g, h, pl.ds(r, rows, stride=dil), :] = (
                    o_ref[r, :, h * HEAD_DIM:(h + 1) * HEAD_DIM].astype(F32))
    l0, l1, l2 = l0_ref[0], l_nat[0], l_nat[1]
    mx = jnp.maximum(jnp.maximum(l0, l1), l2)
    e0, e1, e2 = jnp.exp(l0 - mx), jnp.exp(l1 - mx), jnp.exp(l2 - mx)
    den = e0 + e1 + e2
    a0, a1, a2 = e0 / den, e1 / den, e2 / den
    for h in range(ATTN_HEADS):
        cols = slice(h * HEAD_DIM, (h + 1) * HEAD_DIM)
        acc = (a0[:, h:h + 1] * o0_ref[0, :, cols].astype(F32)
               + a1[:, h:h + 1] * o_nat[0, h]
               + a2[:, h:h + 1] * o_nat[1, h])
        out_ref[:, cols] = acc.astype(out_ref.dtype)


def attn_combine(outs, lses, tm=256):
    S = outs[0].shape[0] * outs[0].shape[1]

    def spec(a):
        dil = a.shape[0]
        return pl.BlockSpec((dil, tm // dil, a.shape[2]), lambda i: (0, i, 0))

    return pl.pallas_call(
        _attn_combine_body,
        grid=(S // tm,),
        in_specs=[spec(a) for a in (*outs, *lses)],
        out_specs=pl.BlockSpec((tm, HEADS_COLS), lambda i: (i, 0)),
        out_shape=_sds((S, HEADS_COLS), BF16),
        scratch_shapes=[pltpu.VMEM((2, ATTN_HEADS, tm, HEAD_DIM), F32), pltpu.VMEM((2, tm, LANES), F32)],
        compiler_params=_cp(("parallel",)),
        name="attn_combine",
    )(*outs, *lses)


def _row_to_col(row, eye):
    return jnp.sum(jnp.where(eye, row, 0.0), axis=1, keepdims=True)


def _chunk_scan(x, L, op, reverse):
    n = x.shape[1]
    pos = lax.broadcasted_iota(I32, x.shape, 1) % L
    d = 1
    while d < L:
        if reverse:
            x = jnp.where(pos < L - d, op(x, pltpu.roll(x, n - d, 1)), x)
        else:
            x = jnp.where(pos >= d, op(x, pltpu.roll(x, d, 1)), x)
        d *= 2
    return x


def _mlstm_gates_body(g_ref, rows_ref, cols_ref, *, L, H):
    x = GATE_SOFTCAP * jnp.tanh(g_ref[...] / GATE_SOFTCAP)
    logf = -(jnp.maximum(-x, 0.0) + jnp.log1p(jnp.exp(-jnp.abs(x))))
    li = jnp.concatenate([x[0:H], x[2 * H:3 * H]], axis=0)
    lf = jnp.concatenate([logf[H:2 * H], logf[3 * H:4 * H]], axis=0)
    bwd = lax.broadcasted_iota(I32, li.shape, 0) >= H
    pre = _chunk_scan(lf, L, jnp.add, False)
    suf = _chunk_scan(lf, L, jnp.add, True)
    total = pre + suf - lf
    b = jnp.where(bwd, suf, pre)
    g = li - b
    pmax = _chunk_scan(g, L, jnp.maximum, False)
    smax = _chunk_scan(g, L, jnp.maximum, True)
    mi = b + jnp.where(bwd, smax, pmax)
    a = total + g
    maxa = total + jnp.maximum(pmax, smax)
    rows_ref[...] = jnp.concatenate([g, a, total, maxa], axis=0)
    n = x.shape[1]
    m = jnp.concatenate([b, a, mi, jnp.zeros((LANES - 6 * H, n), F32)], axis=0)
    for c in range(n // LANES):
        cols_ref[c * LANES:(c + 1) * LANES, :] = m[:, c * LANES:(c + 1) * LANES].T


def mlstm_gates(gates, ts=2048):
    R, S = gates.shape
    H = R // 4
    ts = min(ts, S)
    return pl.pallas_call(
        functools.partial(_mlstm_gates_body, L=MLSTM_CHUNK, H=H),
        grid=(S // ts,),
        in_specs=[pl.BlockSpec((R, ts), lambda i: (0, i))],
        out_specs=[pl.BlockSpec((2 * R, ts), lambda i: (0, i)),
                   pl.BlockSpec((ts, LANES), lambda i: (i, 0))],
        out_shape=[_sds((2 * R, S), F32), _sds((S, LANES), F32)],
        compiler_params=_cp(("parallel",)),
        name="mlstm_gates",
    )(gates)


def _mlstm_chunk(q, k, v, gate, C_ref, n_ref, m_ref, rev, L):
    g_row, a_row, total, maxa, b_col, a_col, mi_col = gate
    q = q * (MLSTM_DQK ** -0.5)
    t_idx = lax.broadcasted_iota(I32, (L, L), 0)
    s_idx = lax.broadcasted_iota(I32, (L, L), 1)
    causal = s_idx >= t_idx if rev else s_idx <= t_idx

    m = m_ref[...]
    C = C_ref[...]
    n = n_ref[...]
    d_intra = jnp.where(causal, b_col + g_row, -jnp.inf)
    d_inter = b_col + m
    m_t = jnp.maximum(d_inter, mi_col)
    w_inter = jnp.exp(d_inter - m_t)
    s = lax.dot_general(q, k, (((1,), (1,)), ((), ())), preferred_element_type=F32) * jnp.exp(d_intra - m_t)
    num = w_inter * _dot(q, C.astype(BF16)) + _dot(s.astype(BF16), v)
    den = (w_inter * jnp.sum(q.astype(F32) * n, axis=1, keepdims=True)
           + jnp.sum(s, axis=1, keepdims=True))
    h = num / jnp.maximum(jnp.abs(den), jnp.exp(-m_t))

    m_new = jnp.maximum(total + m, maxa)
    decay = jnp.exp(total + m - m_new)
    kf = k.astype(F32)
    wkT = kf.T * jnp.exp(a_row - m_new)
    C_ref[...] = decay * C + _dot(wkT.astype(BF16), v)
    n_ref[...] = decay * n + jnp.sum(jnp.exp(a_col - m_new) * kf, axis=0, keepdims=True)
    m_ref[...] = m_new
    return h


def _mlstm_body(qf_ref, kf_ref, vf_ref, rf_ref, cf_ref, qb_ref, kb_ref, vb_ref, rb_ref, cb_ref,
                of_ref, ob_ref, C_ref, n_ref, m_ref, *, L, H):
    @pl.when(pl.program_id(0) == 0)
    def _():
        C_ref[...] = jnp.zeros_like(C_ref)
        n_ref[...] = jnp.zeros_like(n_ref)
        m_ref[...] = jnp.zeros_like(m_ref)

    for hh in range(H):
        qk = slice(hh * MLSTM_DQK, (hh + 1) * MLSTM_DQK)
        vv = slice(hh * MLSTM_DV, (hh + 1) * MLSTM_DV)
        for rev, (q_ref, k_ref, v_ref, r_ref, c_ref, o_ref) in enumerate(
                ((qf_ref, kf_ref, vf_ref, rf_ref, cf_ref, of_ref),
                 (qb_ref, kb_ref, vb_ref, rb_ref, cb_ref, ob_ref))):
            r = rev * H + hh
            nr = 2 * H
            gate = (r_ref[r], r_ref[nr + r], r_ref[2 * nr + r][:, 0:1], r_ref[3 * nr + r][:, 0:1],
                    c_ref[:, r:r + 1], c_ref[:, nr + r:nr + r + 1], c_ref[:, 2 * nr + r:2 * nr + r + 1])
            o_ref[:, vv] = _mlstm_chunk(q_ref[:, qk], k_ref[:, qk], v_ref[:, vv], gate,
                                        C_ref.at[r], n_ref.at[r], m_ref.at[r], bool(rev), L)


def mlstm_scan(proj, rows, cols):
    S = proj.shape[0]
    H, L = MLSTM_HEADS, MLSTM_CHUNK
    NC = S // L
    R = rows.shape[0]
    kv = (2 * H * MLSTM_DQK) // (H * MLSTM_DV)
    r3 = rows.reshape(R, 1, S)

    def specs(rev):
        def ch(c):
            return NC - 1 - c if rev else c
        return [pl.BlockSpec((L, H * MLSTM_DQK), lambda c: (ch(c), 0)),
                pl.BlockSpec((L, H * MLSTM_DQK), lambda c: (ch(c), 1)),
                pl.BlockSpec((L, H * MLSTM_DV), lambda c: (ch(c), kv)),
                pl.BlockSpec((R, 1, L), lambda c: (0, 0, ch(c))),
                pl.BlockSpec((L, LANES), lambda c: (ch(c), 0))]

    return pl.pallas_call(
        functools.partial(_mlstm_body, L=L, H=H),
        grid=(NC,),
        in_specs=specs(0) + specs(1),
        out_specs=[pl.BlockSpec((L, H * MLSTM_DV), lambda c: (c, 0)),
                   pl.BlockSpec((L, H * MLSTM_DV), lambda c: (NC - 1 - c, 0))],
        out_shape=[_sds((S, H * MLSTM_DV), F32), _sds((S, H * MLSTM_DV), F32)],
        scratch_shapes=[pltpu.VMEM((2 * H, MLSTM_DQK, MLSTM_DV), F32),
                        pltpu.VMEM((2 * H, 1, MLSTM_DQK), F32),
                        pltpu.VMEM((2 * H, 1, 1), F32)],
        compiler_params=_cp(("arbitrary",)),
        name="mlstm_scan",
    )(proj, proj, proj, r3, cols, proj, proj, proj, r3, cols)


def _mlstm_out_body(hf_ref, hb_ref, o_ref, hn_ref, y_ref):
    hs = hf_ref[...] + hb_ref[...]
    for h in range(MLSTM_HEADS):
        cols = slice(h * MLSTM_DV, (h + 1) * MLSTM_DV)
        x = hs[:, cols]
        x = x * lax.rsqrt(jnp.mean(x * x, axis=-1, keepdims=True) + RMS_EPS) * hn_ref[:, cols]
        y_ref[:, cols] = (x * jax.nn.sigmoid(o_ref[:, cols].astype(F32))).astype(y_ref.dtype)


def mlstm_out(hf, hb, proj, head_norm, tm=256):
    S, D = hf.shape
    ob = (2 * MLSTM_HEADS * MLSTM_DQK + D) // D
    return pl.pallas_call(
        _mlstm_out_body,
        grid=(S // tm,),
        in_specs=[pl.BlockSpec((tm, D), lambda i: (i, 0)),
                  pl.BlockSpec((tm, D), lambda i: (i, 0)),
                  pl.BlockSpec((tm, D), lambda i: (i, ob)),
                  pl.BlockSpec((1, D), lambda i: (0, 0))],
        out_specs=pl.BlockSpec((tm, D), lambda i: (i, 0)),
        out_shape=_sds((S, D), BF16),
        compiler_params=_cp(("parallel",)),
        name="mlstm_out",
    )(hf, hb, proj, head_norm.reshape(1, D))


def _excl_prefix(mask, U, Ls):
    W = _dot(mask.astype(BF16), U)
    totb = jnp.broadcast_to(W[:, LANES - 1:LANES], W.shape)
    offb = _dot(Ls, totb.astype(BF16))
    return offb + W - mask, W, offb


def _route_body(lg_ref, idx_ref, gate_ref, aff_s, gt_s, eq_s, need_s, *, NCH, cap):
    E = lg_ref.shape[0]
    lg = lg_ref[...]
    mx = jnp.max(lg, axis=0, keepdims=True)
    ex = jnp.exp(lg - mx)
    aff = ex / jnp.sum(ex, axis=0, keepdims=True)
    bits = lax.bitcast_convert_type(aff, I32)

    def count(m):
        return jnp.sum(jnp.sum(m.astype(F32), axis=1, keepdims=True), axis=2, keepdims=True)

    def search(it, T):
        cand = T | lax.shift_left(jnp.int32(1), 30 - it)
        return jnp.where(count(bits >= cand) >= cap, cand, T)

    T = lax.fori_loop(0, 31, search, jnp.zeros((E, 1, 1), I32))
    gt = bits > T
    aff_s[...] = aff
    gt_s[...] = gt.astype(F32)
    eq_s[...] = (bits == T).astype(F32)
    need_s[...] = jnp.broadcast_to(cap - count(gt), need_s.shape)

    r = lax.broadcasted_iota(I32, (LANES, LANES), 0)
    cl = lax.broadcasted_iota(I32, (LANES, LANES), 1)
    U = (r <= cl).astype(BF16)
    r2 = lax.broadcasted_iota(I32, (NCH, NCH), 0)
    c2 = lax.broadcasted_iota(I32, (NCH, NCH), 1)
    Ls = (c2 < r2).astype(BF16)
    j_row = lax.broadcasted_iota(I32, (1, cap), 1).astype(F32)
    c_iota = lax.broadcasted_iota(I32, (NCH, cap), 0).astype(F32)
    l_iota = lax.broadcasted_iota(I32, (LANES, cap), 0).astype(F32)

    def per_expert(e, carry):
        eq = eq_s[e]
        need = need_s[e][0:1, 0:1]
        rank_eq, _, _ = _excl_prefix(eq, U, Ls)
        sel = gt_s[e] + eq * (rank_eq < need).astype(F32)
        _, W, offb = _excl_prefix(sel, U, Ls)
        off_col = offb[:, 0:1]
        incl_col = off_col + W[:, LANES - 1:LANES]
        cidx = jnp.sum((incl_col <= j_row).astype(F32), axis=0, keepdims=True)
        onehot = (c_iota == cidx).astype(F32)
        local = j_row - jnp.sum(off_col * onehot, axis=0, keepdims=True)
        oh = onehot.astype(BF16)
        Wsel = _dot(W.T.astype(BF16), oh)
        lidx = jnp.sum((Wsel <= local).astype(F32), axis=0, keepdims=True)
        idx_ref[e] = (cidx * LANES + lidx).astype(I32)
        at = aff_s[e].T
        hi = at.astype(BF16)
        r1 = at - hi.astype(F32)
        mid = r1.astype(BF16)
        lo = (r1 - mid.astype(F32)).astype(BF16)
        asel = _dot(hi, oh) + _dot(mid, oh) + _dot(lo, oh)
        gate_ref[e] = jnp.sum(jnp.where(l_iota == lidx, asel, 0.0), axis=0, keepdims=True)
        return carry

    lax.fori_loop(0, E, per_expert, 0)


def route(logits_t, cap):
    E, S = logits_t.shape
    NCH = S // LANES
    full = pl.BlockSpec((E, NCH, LANES), lambda: (0, 0, 0))
    ospec = pl.BlockSpec((E, 1, cap), lambda: (0, 0, 0))
    return pl.pallas_call(
        functools.partial(_route_body, NCH=NCH, cap=cap),
        in_specs=[full],
        out_specs=[ospec, ospec],
        out_shape=[_sds((E, 1, cap), I32), _sds((E, 1, cap), F32)],
        scratch_shapes=[pltpu.VMEM((E, NCH, LANES), F32), pltpu.VMEM((E, NCH, LANES), F32),
                        pltpu.VMEM((E, NCH, LANES), F32), pltpu.VMEM((E, 8, LANES), F32)],
        compiler_params=pltpu.CompilerParams(vmem_limit_bytes=VMEM_LIMIT),
        name="route",
    )(logits_t.reshape(E, NCH, LANES))


SUBLANES = 8


def _tiled_rows(x):
    return x.reshape(x.shape[0] // SUBLANES, SUBLANES, x.shape[1])


def _issue_row_copies(idx_ref, tm, make_copy):
    def body(g, carry):
        for k in range(SUBLANES):
            row = idx_ref[0, 0, g * SUBLANES + k]
            make_copy(g, k, lax.shift_right_logical(row, 3), row & (SUBLANES - 1)).start()
        return carry
    lax.fori_loop(0, tm // SUBLANES, body, 0)


def _issue_row_gather(idx_ref, src_hbm, dst, sem, tm):
    _issue_row_copies(idx_ref, tm, lambda g, k, tile, sub: pltpu.make_async_copy(
        src_hbm.at[tile, pl.ds(sub, 1)], dst.at[g, pl.ds(k, 1)], sem))


def _moe_up_body(idx_ref, idxn_ref, hn_hbm, wg_ref, wu_ref, o_ref, buf, sem, *, tm):
    nt = pl.num_programs(1)
    step = pl.program_id(0) * nt + pl.program_id(1)
    nsteps = pl.num_programs(0) * nt
    slot = step % 2

    @pl.when(step == 0)
    def _():
        _issue_row_gather(idx_ref, hn_hbm, buf.at[0], sem.at[0], tm)

    def wait_rows(s):
        pltpu.make_async_copy(hn_hbm.at[pl.ds(0, tm // SUBLANES)], buf.at[s], sem.at[s]).wait()

    wait_rows(slot)
    half = buf.shape[-1]
    kc, rc = 2 * half // MOE_K_CHUNKS, tm // MOE_K_CHUNKS
    g = u = None
    for c in range(MOE_K_CHUNKS):
        pc = (c * kc) % half
        xs = _unpack_bf16_pair(buf[slot, :, :, pc:pc + kc].reshape(tm, kc), high=c * kc >= half)
        gc = _dot(xs, wg_ref[c * kc:(c + 1) * kc, :])
        uc = _dot(xs, wu_ref[c * kc:(c + 1) * kc, :])
        for r in range(c * rc, (c + 1) * rc):
            row = idxn_ref[0, 0, r]
            pltpu.make_async_copy(
                hn_hbm.at[lax.shift_right_logical(row, 3), pl.ds(row & (SUBLANES - 1), 1)],
                buf.at[1 - slot, r // SUBLANES, pl.ds(r % SUBLANES, 1)], sem.at[1 - slot]).start()
        g, u = (gc, uc) if g is None else (g + gc, u + uc)
    o_ref[...] = (g * jax.nn.sigmoid(g) * u).astype(o_ref.dtype)

    @pl.when(step == nsteps - 1)
    def _():
        wait_rows(1 - slot)


def moe_up(hn, idx, wg, wu, layer, tm):
    E, _, cap = idx.shape
    D, F = wg.shape[2], wg.shape[3]
    assert hn.shape[1] * 2 == D and hn.dtype == jnp.uint32
    nt = cap // tm
    nsteps = E * nt

    def nxt(e, t):
        s = jnp.minimum(e * nt + t + 1, nsteps - 1)
        return (s // nt, 0, s % nt)

    wspec = pl.BlockSpec((None, None, D, F), lambda e, t: (layer, e, 0, 0))
    return pl.pallas_call(
        functools.partial(_moe_up_body, tm=tm),
        grid=(E, nt),
        in_specs=[pl.BlockSpec((1, 1, tm), lambda e, t: (e, 0, t), memory_space=pltpu.SMEM),
                  pl.BlockSpec((1, 1, tm), nxt, memory_space=pltpu.SMEM),
                  pl.BlockSpec(memory_space=pl.ANY),
                  wspec, wspec],
        out_specs=pl.BlockSpec((tm, F), lambda e, t: (e * nt + t, 0)),
        out_shape=_sds((E * cap, F), BF16),
        scratch_shapes=[pltpu.VMEM((2, tm // SUBLANES, SUBLANES, D // 2), jnp.uint32),
                        pltpu.SemaphoreType.DMA((2,))],
        compiler_params=_cp(("arbitrary", "arbitrary")),
        name="moe_up",
    )(idx, idx, _tiled_rows(hn), wg, wu)


DOWN_SLOTS = 3


def _moe_down_body(idx_ref, idxn_ref, gate_ref, h_ref, wd_ref, x_hbm, xo_hbm, xbuf, gsem, ssem, *, tm):
    del x_hbm
    nt = pl.num_programs(1)
    t = pl.program_id(1)
    step = pl.program_id(0) * nt + t
    nsteps = pl.num_programs(0) * nt
    cur = step % DOWN_SLOTS
    nxt = (step + 1) % DOWN_SLOTS
    prv = (step + 2) % DOWN_SLOTS

    def wait_scatter(slot):
        pltpu.make_async_copy(xbuf.at[slot], xo_hbm.at[pl.ds(0, tm // SUBLANES)], ssem.at[slot]).wait()

    @pl.when((step >= 2) & (t != 1))
    def _():
        wait_scatter(nxt)

    @pl.when((t == 0) & (step >= 1))
    def _():
        wait_scatter(prv)

    @pl.when(t == 0)
    def _():
        _issue_row_gather(idx_ref, xo_hbm, xbuf.at[cur], gsem.at[cur], tm)

    @pl.when(t + 1 < nt)
    def _():
        _issue_row_gather(idxn_ref, xo_hbm, xbuf.at[nxt], gsem.at[nxt], tm)

    y = _dot(h_ref[...], wd_ref[...])
    t_idx = lax.broadcasted_iota(I32, (tm, tm), 0)
    s_idx = lax.broadcasted_iota(I32, (tm, tm), 1)
    gate_col = _row_to_col(gate_ref[0], t_idx == s_idx)
    pltpu.make_async_copy(xo_hbm.at[pl.ds(0, tm // SUBLANES)], xbuf.at[cur], gsem.at[cur]).wait()
    upd = xbuf[cur].reshape(y.shape) + y * gate_col
    xbuf[cur] = upd.reshape(xbuf.shape[1:])
    _issue_row_copies(idx_ref, tm, lambda g, k, tile, sub: pltpu.make_async_copy(
        xbuf.at[cur, g, pl.ds(k, 1)], xo_hbm.at[tile, pl.ds(sub, 1)], ssem.at[cur]))

    @pl.when(step == nsteps - 1)
    def _():
        wait_scatter(prv)
        wait_scatter(cur)


def moe_down(x, hmid, idx, gate, wd, layer, tm):
    E, _, cap = idx.shape
    S, D = x.shape
    F = wd.shape[2]
    nt = cap // tm
    assert nt >= 2, "the scatter wait schedule needs at least two tiles per expert"
    nsteps = E * nt

    def nxt(e, t):
        s = jnp.minimum(e * nt + t + 1, nsteps - 1)
        return (s // nt, 0, s % nt)

    return pl.pallas_call(
        functools.partial(_moe_down_body, tm=tm),
        grid=(E, nt),
        in_specs=[pl.BlockSpec((1, 1, tm), lambda e, t: (e, 0, t), memory_space=pltpu.SMEM),
                  pl.BlockSpec((1, 1, tm), nxt, memory_space=pltpu.SMEM),
                  pl.BlockSpec((None, 1, tm), lambda e, t: (e, 0, t)),
                  pl.BlockSpec((tm, F), lambda e, t: (e * nt + t, 0)),
                  pl.BlockSpec((None, None, F, D), lambda e, t: (layer, e, 0, 0)),
                  pl.BlockSpec(memory_space=pl.ANY)],
        out_specs=pl.BlockSpec(memory_space=pl.ANY),
        out_shape=_sds((S // SUBLANES, SUBLANES, D), F32),
        input_output_aliases={5: 0},
        scratch_shapes=[pltpu.VMEM((DOWN_SLOTS, tm // SUBLANES, SUBLANES, D), F32),
                        pltpu.SemaphoreType.DMA((DOWN_SLOTS,)),
                        pltpu.SemaphoreType.DMA((DOWN_SLOTS,))],
        compiler_params=_cp(("arbitrary", "arbitrary")),
        name="moe_down",
    )(idx, idx, gate, hmid, wd, _tiled_rows(x)).reshape(S, D)


def expert_choice_ffn(x, norm_g, w_router, w_gate, w_up, w_down, layer):
    S = x.shape[0]
    E = w_router.shape[1]
    cap = CAPACITY_FACTOR * S // E
    hn, logits_t = rmsnorm_proj(x, norm_g, w_router, jnp.zeros((E,), F32), None, pack=True)
    idx, gate = route(logits_t, cap)
    tm = min(MOE_TILE, cap // 2)
    hmid = moe_up(hn, idx, w_gate, w_up, layer, tm)
    return moe_down(x, hmid, idx, gate, w_down, layer, tm)


def conformer_mixer(x, norm_g, w_in, b_in, w_dw, b_dw, ln_g, ln_b, w_out, b_out, j):
    h = rmsnorm(x, norm_g, BF16)
    u = matmul_glu(h, w_in, j, b_in)
    u = dwconv_ln_silu(u, w_dw, b_dw, ln_g, ln_b)
    return matmul_bias_residual(u, w_out, j, b_out, x)


def attention_mixer(x, norm_g, positions, w_qkv, w_out, j):
    h = rmsnorm(x, norm_g, BF16)
    tabs = rope_tables(positions)
    outs, lses = [], []
    for g, (window, dil) in enumerate(DIL_PATTERNS):
        assert window // (2 * dil) == ATTN_HALF
        o, lse = band_attention(matmul_qkv_rope(h, w_qkv, tabs, g, dil), dil)
        outs.append(o)
        lses.append(lse)
    o = attn_combine(outs, lses)
    return matmul_bias_residual(o, w_out, j, jnp.zeros((w_out.shape[2],), F32), x)


def mlstm_mixer(x, norm_g, w_in, b_gates, head_norm, w_out, j):
    D = x.shape[1]
    n_main = 2 * MLSTM_HEADS * MLSTM_DQK + 2 * D
    h, gates = rmsnorm_proj(x, norm_g, w_in[:, n_main:], b_gates, BF16)
    proj = matmul_plain(h, w_in[:, :n_main].astype(BF16), BF16)
    hf, hb = mlstm_scan(proj, *mlstm_gates(gates))
    y = mlstm_out(hf, hb, proj, head_norm)
    return matmul_bias_residual(y, w_out, j, jnp.zeros((D,), F32), x)


def kernel(x, positions, mix_norm, ffn_norm, conv_w_in, conv_b_in, conv_w_dw, conv_b_dw, conv_ln_g,
           conv_ln_b, conv_w_out, conv_b_out, attn_w_qkv, attn_w_out, mlstm_w_in, mlstm_b_gates,
           mlstm_head_norm, mlstm_w_out, router_w, moe_w_gate, moe_w_up, moe_w_down, final_norm):
    assert x.shape[0] == 1, "single-sequence batch"
    xs = x[0]
    pos = positions[0]
    depth = mix_norm.shape[0]
    conv_w_in, conv_w_out = conv_w_in.astype(BF16), conv_w_out.astype(BF16)
    attn_w_out, mlstm_w_out = attn_w_out.astype(BF16), mlstm_w_out.astype(BF16)
    moe_w_gate, moe_w_up, moe_w_down = (w.astype(BF16) for w in (moe_w_gate, moe_w_up, moe_w_down))
    for i in range(depth):
        kind, j = i % 3, i // 3
        if kind == 0:
            xs = conformer_mixer(xs, mix_norm[i], conv_w_in, conv_b_in[j], conv_w_dw[j], conv_b_dw[j],
                                 conv_ln_g[j], conv_ln_b[j], conv_w_out, conv_b_out[j], j)
        elif kind == 1:
            xs = attention_mixer(xs, mix_norm[i], pos, attn_w_qkv[j].astype(BF16), attn_w_out, j)
        else:
            xs = mlstm_mixer(xs, mix_norm[i], mlstm_w_in[j], mlstm_b_gates[j], mlstm_head_norm[j],
                             mlstm_w_out, j)
        xs = expert_choice_ffn(xs, ffn_norm[i], router_w[i], moe_w_gate, moe_w_up, moe_w_down, i)
    return rmsnorm(xs, final_norm, F32)[None]
```

```python
import functools

import jax
import jax.numpy as jnp
from jax import lax
from jax.experimental import pallas as pl
from jax.experimental.pallas import tpu as pltpu

F32 = jnp.float32
BF16 = jnp.bfloat16
I32 = jnp.int32

RMS_EPS = 1e-6
CONV_WIDTH = 31
CONV_HALO = 16
HEAD_DIM = 128
ATTN_HEADS = 16
DIL_PATTERNS = ((128, 1), (512, 4), (2048, 16))
ATTN_HALF = 64
ROT_DIM = 32
ROPE_THETA = 500000.0
GROUP_COLS = 3 * ATTN_HEADS * HEAD_DIM
HEADS_COLS = ATTN_HEADS * HEAD_DIM
MLSTM_HEADS = 8
MLSTM_DQK = 256
MLSTM_DV = 512
MLSTM_CHUNK = 128
GATE_SOFTCAP = 15.0
N_EXPERTS = 16
MOE_TILE = 512
MOE_K_CHUNKS = 8
CAPACITY_FACTOR = 2
LANES = 128
MXU_COLS = 256
VMEM_LIMIT = 56 * 1024 * 1024


def _cp(sem, vmem=VMEM_LIMIT):
    return pltpu.CompilerParams(dimension_semantics=sem, vmem_limit_bytes=vmem)


def _sds(shape, dtype):
    return jax.ShapeDtypeStruct(shape, dtype)


def _rmsnorm_body(x_ref, g_ref, o_ref):
    x = x_ref[...]
    ms = jnp.mean(x * x, axis=-1, keepdims=True)
    o_ref[...] = (x * lax.rsqrt(ms + RMS_EPS) * g_ref[...]).astype(o_ref.dtype)


def rmsnorm(x, g, out_dtype, tm=256):
    S, D = x.shape
    return pl.pallas_call(
        _rmsnorm_body,
        grid=(S // tm,),
        in_specs=[pl.BlockSpec((tm, D), lambda i: (i, 0)),
                  pl.BlockSpec((1, D), lambda i: (0, 0))],
        out_specs=pl.BlockSpec((tm, D), lambda i: (i, 0)),
        out_shape=_sds((S, D), out_dtype),
        compiler_params=_cp(("parallel",)),
        name="rmsnorm",
    )(x, g.reshape(1, D))


def _rmsnorm_proj_body(x_ref, g_ref, wt_ref, b_ref, o_ref, p_ref, *, pack):
    x = x_ref[...]
    ms = jnp.mean(x * x, axis=-1, keepdims=True)
    y = x * lax.rsqrt(ms + RMS_EPS) * g_ref[...]
    y_hi = y.astype(BF16)
    y_hi32 = y_hi.astype(F32)
    if pack:
        half = y.shape[1] // 2
        bits = lax.bitcast_convert_type(y_hi32, jnp.uint32)
        o_ref[...] = ((bits[:, half:] & jnp.uint32(0xFFFF0000))
                      | lax.shift_right_logical(bits[:, :half], jnp.uint32(16)))
    else:
        o_ref[...] = y_hi if o_ref.dtype == BF16 else y.astype(o_ref.dtype)
    y_lo = (y - y_hi32).astype(BF16)
    n = p_ref.shape[0]
    nt_dims = (((1,), (1,)), ((), ()))
    r = lax.dot_general(wt_ref[...], y_hi, nt_dims, preferred_element_type=F32)
    r2 = lax.dot_general(wt_ref[0:n, :], y_lo, nt_dims, preferred_element_type=F32)
    p_ref[...] = r[0:n] + r[n:2 * n] + r2 + b_ref[...]


def _unpack_bf16_pair(p, high):
    bits = p & jnp.uint32(0xFFFF0000) if high else lax.shift_left(p, jnp.uint32(16))
    return lax.bitcast_convert_type(bits, F32).astype(BF16)


def rmsnorm_proj(x, g, w, b, out_dtype, pack=False, tm=256):
    S, D = x.shape
    N = w.shape[1]
    Do, out_dtype = (D // 2, jnp.uint32) if pack else (D, out_dtype)
    wt = w.T
    w_hi = wt.astype(BF16)
    w_split = jnp.concatenate([w_hi, (wt - w_hi.astype(F32)).astype(BF16)], axis=0)
    return pl.pallas_call(
        functools.partial(_rmsnorm_proj_body, pack=pack),
        grid=(S // tm,),
        in_specs=[pl.BlockSpec((tm, D), lambda i: (i, 0)),
                  pl.BlockSpec((1, D), lambda i: (0, 0)),
                  pl.BlockSpec((2 * N, D), lambda i: (0, 0)),
                  pl.BlockSpec((N, 1), lambda i: (0, 0))],
        out_specs=[pl.BlockSpec((tm, Do), lambda i: (i, 0)),
                   pl.BlockSpec((N, tm), lambda i: (0, i))],
        out_shape=[_sds((S, Do), out_dtype), _sds((N, S), F32)],
        compiler_params=_cp(("parallel",)),
        name="rmsnorm_proj",
    )(x, g.reshape(1, D), w_split, b.reshape(N, 1))


def _dot(a, b):
    return jnp.dot(a, b, preferred_element_type=F32)


def _mm_plain_body(a_ref, b_ref, o_ref):
    o_ref[...] = _dot(a_ref[...], b_ref[...]).astype(o_ref.dtype)


def matmul_plain(a, b, out_dtype, tm=512, tn=1024):
    M, K = a.shape
    N = b.shape[1]
    return pl.pallas_call(
        _mm_plain_body,
        grid=(N // tn, M // tm),
        in_specs=[pl.BlockSpec((tm, K), lambda j, i: (i, 0)),
                  pl.BlockSpec((K, tn), lambda j, i: (0, j))],
        out_specs=pl.BlockSpec((tm, tn), lambda j, i: (i, j)),
        out_shape=_sds((M, N), out_dtype),
        compiler_params=_cp(("parallel", "parallel")),
        name="matmul_plain",
    )(a, b)


def _mm_res_body(a_ref, b_ref, bias_ref, r_ref, o_ref):
    o_ref[...] = _dot(a_ref[...], b_ref[...]) + bias_ref[...] + r_ref[...]


def matmul_bias_residual(a, b, layer, bias, res, tm=512, tn=1024):
    M, K = a.shape
    N = b.shape[2]
    return pl.pallas_call(
        _mm_res_body,
        grid=(N // tn, M // tm),
        in_specs=[pl.BlockSpec((tm, K), lambda j, i: (i, 0)),
                  pl.BlockSpec((None, K, tn), lambda j, i: (layer, 0, j)),
                  pl.BlockSpec((1, tn), lambda j, i: (0, j)),
                  pl.BlockSpec((tm, tn), lambda j, i: (i, j))],
        out_specs=pl.BlockSpec((tm, tn), lambda j, i: (i, j)),
        out_shape=_sds((M, N), F32),
        compiler_params=_cp(("parallel", "parallel")),
        name="matmul_bias_residual",
    )(a, b, bias.reshape(1, N), res)


def _mm_glu_body(a_ref, b1_ref, b2_ref, bias1_ref, bias2_ref, o_ref):
    a = a_ref[...]
    u1 = _dot(a, b1_ref[...]) + bias1_ref[...]
    u2 = _dot(a, b2_ref[...]) + bias2_ref[...]
    o_ref[...] = u1 * jax.nn.sigmoid(u2)


def matmul_glu(a, b, layer, bias, tm=512, tn=1024):
    M, K = a.shape
    N2 = b.shape[2] // 2
    nb = N2 // tn
    bias = bias.reshape(1, 2 * N2)
    return pl.pallas_call(
        _mm_glu_body,
        grid=(nb, M // tm),
        in_specs=[pl.BlockSpec((tm, K), lambda j, i: (i, 0)),
                  pl.BlockSpec((None, K, tn), lambda j, i: (layer, 0, j)),
                  pl.BlockSpec((None, K, tn), lambda j, i: (layer, 0, j + nb)),
                  pl.BlockSpec((1, tn), lambda j, i: (0, j)),
                  pl.BlockSpec((1, tn), lambda j, i: (0, j + nb))],
        out_specs=pl.BlockSpec((tm, tn), lambda j, i: (i, j)),
        out_shape=_sds((M, N2), F32),
        compiler_params=_cp(("parallel", "parallel")),
        name="matmul_glu",
    )(a, b, b, bias, bias)


def _mm_rope_body(a_ref, b_ref, c_ref, s1_ref, s2_ref, o_ref, stage, *, tn, tiles_per_kind, dil):
    kind = pl.program_id(0) // tiles_per_kind
    rot = kind < 2
    scale = jnp.where(kind == 0, HEAD_DIM ** -0.5, 1.0).astype(F32)
    c = jnp.where(rot, c_ref[...], 1.0) * scale
    s1 = jnp.where(rot, s1_ref[...], 0.0) * scale
    s2 = jnp.where(rot, s2_ref[...], 0.0) * scale
    a = a_ref[...]
    rows = a.shape[0] // dil
    for ch in range(tn // MXU_COLS):
        acc = _dot(a, b_ref[:, ch * MXU_COLS:(ch + 1) * MXU_COLS])
        for hh in range(MXU_COLS // HEAD_DIM):
            hb = ch * (MXU_COLS // HEAD_DIM) + hh
            cols = slice(hb * HEAD_DIM, (hb + 1) * HEAD_DIM)
            t = acc[:, hh * HEAD_DIM:(hh + 1) * HEAD_DIM]
            val = (t * c + pltpu.roll(t, ROT_DIM // 2, 1) * s1
                   + pltpu.roll(t, HEAD_DIM - ROT_DIM // 2, 1) * s2)
            if dil == 1:
                o_ref[0, :, cols] = val.astype(o_ref.dtype)
            else:
                stage[hb] = val
                for r in range(dil):
                    o_ref[r, :, cols] = stage[hb, pl.ds(r, rows, stride=dil), :].astype(o_ref.dtype)


def matmul_qkv_rope(a, b, tabs, group, dil, tm=512, tn=2048):
    M, K = a.shape
    c, s1, s2 = tabs
    nj = GROUP_COLS // tn
    tab_spec = pl.BlockSpec((tm, HEAD_DIM), lambda j, i: (i, 0))
    return pl.pallas_call(
        functools.partial(_mm_rope_body, tn=tn, tiles_per_kind=HEADS_COLS // tn, dil=dil),
        grid=(nj, M // tm),
        in_specs=[pl.BlockSpec((tm, K), lambda j, i: (i, 0)),
                  pl.BlockSpec((K, tn), lambda j, i: (0, group * nj + j)),
                  tab_spec, tab_spec, tab_spec],
        out_specs=pl.BlockSpec((dil, tm // dil, tn), lambda j, i: (0, i, j)),
        out_shape=_sds((dil, M // dil, GROUP_COLS), BF16),
        scratch_shapes=[pltpu.VMEM((tn // HEAD_DIM, tm, HEAD_DIM) if dil > 1 else (1, 8, LANES), F32)],
        compiler_params=_cp(("parallel", "parallel")),
        name=f"matmul_qkv_rope_d{dil}",
    )(a, b, c, s1, s2)


def _rope_tab_body(pos_ref, invf_ref, c_ref, s1_ref, s2_ref):
    half = ROT_DIM // 2
    ang = pos_ref[...].astype(F32) * invf_ref[...]
    cos, sin = jnp.cos(ang), jnp.sin(ang)
    lane = lax.broadcasted_iota(I32, ang.shape, 1)
    c_ref[...] = jnp.where(lane < ROT_DIM, cos, 1.0)
    s1_ref[...] = jnp.where((lane >= half) & (lane < ROT_DIM), sin, 0.0)
    s2_ref[...] = jnp.where(lane < half, -sin, 0.0)


def rope_tables(positions, tm=512):
    S = positions.shape[0]
    half = ROT_DIM // 2
    inv_freq = ROPE_THETA ** (-jnp.arange(half, dtype=F32) / half)
    invf = jnp.zeros((1, HEAD_DIM), F32).at[0, :ROT_DIM].set(jnp.tile(inv_freq, 2))
    spec = pl.BlockSpec((tm, HEAD_DIM), lambda i: (i, 0))
    return pl.pallas_call(
        _rope_tab_body,
        grid=(S // tm,),
        in_specs=[pl.BlockSpec((tm, 1), lambda i: (i, 0)),
                  pl.BlockSpec((1, HEAD_DIM), lambda i: (0, 0))],
        out_specs=[spec, spec, spec],
        out_shape=[_sds((S, HEAD_DIM), F32)] * 3,
        compiler_params=_cp(("parallel",)),
        name="rope_tables",
    )(positions.reshape(S, 1), invf)


def _dwconv_ln_body(up_ref, um_ref, un_ref, w_ref, b_ref, g_ref, beta_ref, o_ref, pad, cv, shifted,
                    *, ts, lanes):
    i = pl.program_id(0)
    last = pl.num_programs(0) - 1
    H = CONV_HALO
    D = um_ref.shape[1]
    pad[0:H, :] = jnp.where(i > 0, up_ref[...], 0.0)
    pad[H:H + ts, :] = um_ref[...]
    pad[H + ts:H + ts + H, :] = jnp.where(i < last, un_ref[...], 0.0)
    off = H - CONV_WIDTH // 2
    for l0 in range(0, D, lanes):
        cols = slice(l0, l0 + lanes)
        out = None
        for b in range(8):
            vb = None
            for a in range(-(-(off + CONV_WIDTH) // 8)):
                k = 8 * a + b - off
                if 0 <= k < CONV_WIDTH:
                    term = pad[8 * a:8 * a + ts + 8, cols] * w_ref[k:k + 1, cols]
                    vb = term if vb is None else vb + term
            shifted[b] = vb
            sh = shifted[b, b:b + ts, :]
            out = sh if out is None else out + sh
        cv[:, cols] = out
    u = cv[...] + b_ref[...]
    mu = jnp.mean(u, axis=-1, keepdims=True)
    uc = u - mu
    y = uc * lax.rsqrt(jnp.mean(uc * uc, axis=-1, keepdims=True) + RMS_EPS)
    y = y * g_ref[...] + beta_ref[...]
    o_ref[...] = (y * jax.nn.sigmoid(y)).astype(o_ref.dtype)


def dwconv_ln_silu(u, w, b, g, beta, ts=128, lanes=512):
    S, D = u.shape
    H = CONV_HALO
    nh = ts // H
    vec = pl.BlockSpec((1, D), lambda i: (0, 0))
    return pl.pallas_call(
        functools.partial(_dwconv_ln_body, ts=ts, lanes=lanes),
        grid=(S // ts,),
        in_specs=[pl.BlockSpec((H, D), lambda i: (jnp.maximum(i * nh - 1, 0), 0)),
                  pl.BlockSpec((ts, D), lambda i: (i, 0)),
                  pl.BlockSpec((H, D), lambda i: (jnp.minimum((i + 1) * nh, S // H - 1), 0)),
                  pl.BlockSpec((CONV_WIDTH, D), lambda i: (0, 0)),
                  vec, vec, vec],
        out_specs=pl.BlockSpec((ts, D), lambda i: (i, 0)),
        out_shape=_sds((S, D), BF16),
        scratch_shapes=[pltpu.VMEM((ts + 2 * H, D), F32), pltpu.VMEM((ts, D), F32),
                        pltpu.VMEM((8, ts + 8, lanes), F32)],
        compiler_params=_cp(("parallel",)),
        name="dwconv_ln_silu",
    )(u, u, u, w, b.reshape(1, D), g.reshape(1, D), beta.reshape(1, D))


ATTN_SUB = 128


def _band_attn_body(q_ref, kp_ref, km_ref, kn_ref, vp_ref, vm_ref, vn_ref, o_ref, lse_ref, kc, vc,
                    *, tq, L):
    i = pl.program_id(1)
    Hh = ATTN_HALF
    kc[0:Hh, :] = kp_ref[...]
    kc[Hh:Hh + tq, :] = km_ref[...]
    kc[Hh + tq:Hh + tq + Hh, :] = kn_ref[...]
    vc[0:Hh, :] = vp_ref[...]
    vc[Hh:Hh + tq, :] = vm_ref[...]
    vc[Hh + tq:Hh + tq + Hh, :] = vn_ref[...]
    nk = ATTN_SUB + 2 * Hh
    t = lax.broadcasted_iota(I32, (ATTN_SUB, nk), 0)
    c = lax.broadcasted_iota(I32, (ATTN_SUB, nk), 1)
    lane_h = lax.broadcasted_iota(I32, (ATTN_SUB, LANES), 1)
    for s in range(tq // ATTN_SUB):
        row0 = s * ATTN_SUB
        kpos = i * tq + (row0 - Hh) + c
        mask = (c >= t) & (c <= t + 2 * Hh) & (kpos >= 0) & (kpos < L)
        lse_tile = jnp.zeros((ATTN_SUB, LANES), F32)
        for h in range(ATTN_HEADS):
            cols = slice(h * HEAD_DIM, (h + 1) * HEAD_DIM)
            q = q_ref[row0:row0 + ATTN_SUB, cols]
            k = kc[row0:row0 + nk, cols]
            v = vc[row0:row0 + nk, cols]
            logits = lax.dot_general(q, k, (((1,), (1,)), ((), ())), preferred_element_type=F32)
            logits = jnp.where(mask, logits, -jnp.inf)
            mx = jnp.max(logits, axis=-1, keepdims=True)
            p = jnp.exp(logits - mx)
            l = jnp.sum(p, axis=-1, keepdims=True)
            o = _dot(p.astype(BF16), v) / l
            o_ref[row0:row0 + ATTN_SUB, cols] = o.astype(o_ref.dtype)
            lse_tile = jnp.where(lane_h == h, mx + jnp.log(l), lse_tile)
        lse_ref[row0:row0 + ATTN_SUB, :] = lse_tile


def band_attention(qkv, dil, tq=256):
    _, L, C = qkv.shape
    S = dil * L
    tq = min(tq, L)
    x2 = qkv.reshape(S, C)
    nq = L // tq
    hb = tq // ATTN_HALF
    nhalo = S // ATTN_HALF

    def main(kind):
        return pl.BlockSpec((tq, HEADS_COLS), lambda r, i: (r * nq + i, kind))

    def prev(kind):
        return pl.BlockSpec((ATTN_HALF, HEADS_COLS),
                            lambda r, i: (jnp.maximum((r * nq + i) * hb - 1, 0), kind))

    def nxt(kind):
        return pl.BlockSpec((ATTN_HALF, HEADS_COLS),
                            lambda r, i: (jnp.minimum((r * nq + i + 1) * hb, nhalo - 1), kind))

    o, lse = pl.pallas_call(
        functools.partial(_band_attn_body, tq=tq, L=L),
        grid=(dil, nq),
        in_specs=[main(0), prev(1), main(1), nxt(1), prev(2), main(2), nxt(2)],
        out_specs=[pl.BlockSpec((tq, HEADS_COLS), lambda r, i: (r * nq + i, 0)),
                   pl.BlockSpec((tq, LANES), lambda r, i: (r * nq + i, 0))],
        out_shape=[_sds((S, HEADS_COLS), BF16), _sds((S, LANES), F32)],
        scratch_shapes=[pltpu.VMEM((tq + 2 * ATTN_HALF, HEADS_COLS), BF16),
                        pltpu.VMEM((tq + 2 * ATTN_HALF, HEADS_COLS), BF16)],
        compiler_params=_cp(("parallel", "parallel")),
        name=f"band_attention_d{dil}",
    )(x2, x2, x2, x2, x2, x2, x2)
    return o.reshape(dil, L, HEADS_COLS), lse.reshape(dil, L, LANES)


def _attn_combine_body(o0_ref, o1_ref, o2_ref, l0_ref, l1_ref, l2_ref, out_ref, o_nat, l_nat):
    for g, (o_ref, l_ref) in enumerate(((o1_ref, l1_ref), (o2_ref, l2_ref))):
        dil, rows = o_ref.shape[0], o_ref.shape[1]
        for r in range(dil):
            l_nat[g, pl.ds(r, rows, stride=dil), :] = l_ref[r]
            for h in range(ATTN_HEADS):
                o_nat[g, h, pl.ds(r, rows, stride=dil), :] = (
                    o_ref[r, :, h * HEAD_DIM:(h + 1) * HEAD_DIM].astype(F32))
    l0, l1, l2 = l0_ref[0], l_nat[0], l_nat[1]
    mx = jnp.maximum(jnp.maximum(l0, l1), l2)
    e0, e1, e2 = jnp.exp(l0 - mx), jnp.exp(l1 - mx), jnp.exp(l2 - mx)
    den = e0 + e1 + e2
    a0, a1, a2 = e0 / den, e1 / den, e2 / den
    for h in range(ATTN_HEADS):
        cols = slice(h * HEAD_DIM, (h + 1) * HEAD_DIM)
        acc = (a0[:, h:h + 1] * o0_ref[0, :, cols].astype(F32)
               + a1[:, h:h + 1] * o_nat[0, h]
               + a2[:, h:h + 1] * o_nat[1, h])
        out_ref[:, cols] = acc.astype(out_ref.dtype)


def attn_combine(outs, lses, tm=256):
    S = outs[0].shape[0] * outs[0].shape[1]

    def spec(a):
        dil = a.shape[0]
        return pl.BlockSpec((dil, tm // dil, a.shape[2]), lambda i: (0, i, 0))

    return pl.pallas_call(
        _attn_combine_body,
        grid=(S // tm,),
        in_specs=[spec(a) for a in (*outs, *lses)],
        out_specs=pl.BlockSpec((tm, HEADS_COLS), lambda i: (i, 0)),
        out_shape=_sds((S, HEADS_COLS), BF16),
        scratch_shapes=[pltpu.VMEM((2, ATTN_HEADS, tm, HEAD_DIM), F32), pltpu.VMEM((2, tm, LANES), F32)],
        compiler_params=_cp(("parallel",)),
        name="attn_combine",
    )(*outs, *lses)


def _row_to_col(row, eye):
    return jnp.sum(jnp.where(eye, row, 0.0), axis=1, keepdims=True)


def _chunk_scan(x, L, op, reverse):
    n = x.shape[1]
    pos = lax.broadcasted_iota(I32, x.shape, 1) % L
    d = 1
    while d < L:
        if reverse:
            x = jnp.where(pos < L - d, op(x, pltpu.roll(x, n - d, 1)), x)
        else:
            x = jnp.where(pos >= d, op(x, pltpu.roll(x, d, 1)), x)
        d *= 2
    return x


def _mlstm_gates_body(g_ref, rows_ref, cols_ref, *, L, H):
    x = GATE_SOFTCAP * jnp.tanh(g_ref[...] / GATE_SOFTCAP)
    logf = -(jnp.maximum(-x, 0.0) + jnp.log1p(jnp.exp(-jnp.abs(x))))
    li = jnp.concatenate([x[0:H], x[2 * H:3 * H]], axis=0)
    lf = jnp.concatenate([logf[H:2 * H], logf[3 * H:4 * H]], axis=0)
    bwd = lax.broadcasted_iota(I32, li.shape, 0) >= H
    pre = _chunk_scan(lf, L, jnp.add, False)
    suf = _chunk_scan(lf, L, jnp.add, True)
    total = pre + suf - lf
    b = jnp.where(bwd, suf, pre)
    g = li - b
    pmax = _chunk_scan(g, L, jnp.maximum, False)
    smax = _chunk_scan(g, L, jnp.maximum, True)
    mi = b + jnp.where(bwd, smax, pmax)
    a = total + g
    maxa = total + jnp.maximum(pmax, smax)
    rows_ref[...] = jnp.concatenate([g, a, total, maxa], axis=0)
    n = x.shape[1]
    m = jnp.concatenate([b, a, mi, jnp.zeros((LANES - 6 * H, n), F32)], axis=0)
    for c in range(n // LANES):
        cols_ref[c * LANES:(c + 1) * LANES, :] = m[:, c * LANES:(c + 1) * LANES].T


def mlstm_gates(gates, ts=2048):
    R, S = gates.shape
    H = R // 4
    ts = min(ts, S)
    return pl.pallas_call(
        functools.partial(_mlstm_gates_body, L=MLSTM_CHUNK, H=H),
        grid=(S // ts,),
        in_specs=[pl.BlockSpec((R, ts), lambda i: (0, i))],
        out_specs=[pl.BlockSpec((2 * R, ts), lambda i: (0, i)),
                   pl.BlockSpec((ts, LANES), lambda i: (i, 0))],
        out_shape=[_sds((2 * R, S), F32), _sds((S, LANES), F32)],
        compiler_params=_cp(("parallel",)),
        name="mlstm_gates",
    )(gates)


def _mlstm_chunk(q, k, v, gate, C_ref, n_ref, m_ref, rev, L):
    g_row, a_row, total, maxa, b_col, a_col, mi_col = gate
    q = q * (MLSTM_DQK ** -0.5)
    t_idx = lax.broadcasted_iota(I32, (L, L), 0)
    s_idx = lax.broadcasted_iota(I32, (L, L), 1)
    causal = s_idx >= t_idx if rev else s_idx <= t_idx

    m = m_ref[...]
    C = C_ref[...]
    n = n_ref[...]
    d_intra = jnp.where(causal, b_col + g_row, -jnp.inf)
    d_inter = b_col + m
    m_t = jnp.maximum(d_inter, mi_col)
    w_inter = jnp.exp(d_inter - m_t)
    s = lax.dot_general(q, k, (((1,), (1,)), ((), ())), preferred_element_type=F32) * jnp.exp(d_intra - m_t)
    num = w_inter * _dot(q, C.astype(BF16)) + _dot(s.astype(BF16), v)
    den = (w_inter * jnp.sum(q.astype(F32) * n, axis=1, keepdims=True)
           + jnp.sum(s, axis=1, keepdims=True))
    h = num / jnp.maximum(jnp.abs(den), jnp.exp(-m_t))

    m_new = jnp.maximum(total + m, maxa)
    decay = jnp.exp(total + m - m_new)
    kf = k.astype(F32)
    wkT = kf.T * jnp.exp(a_row - m_new)
    C_ref[...] = decay * C + _dot(wkT.astype(BF16), v)
    n_ref[...] = decay * n + jnp.sum(jnp.exp(a_col - m_new) * kf, axis=0, keepdims=True)
    m_ref[...] = m_new
    return h


def _mlstm_body(qf_ref, kf_ref, vf_ref, rf_ref, cf_ref, qb_ref, kb_ref, vb_ref, rb_ref, cb_ref,
                of_ref, ob_ref, C_ref, n_ref, m_ref, *, L, H):
    @pl.when(pl.program_id(0) == 0)
    def _():
        C_ref[...] = jnp.zeros_like(C_ref)
        n_ref[...] = jnp.zeros_like(n_ref)
        m_ref[...] = jnp.zeros_like(m_ref)

    for hh in range(H):
        qk = slice(hh * MLSTM_DQK, (hh + 1) * MLSTM_DQK)
        vv = slice(hh * MLSTM_DV, (hh + 1) * MLSTM_DV)
        for rev, (q_ref, k_ref, v_ref, r_ref, c_ref, o_ref) in enumerate(
                ((qf_ref, kf_ref, vf_ref, rf_ref, cf_ref, of_ref),
                 (qb_ref, kb_ref, vb_ref, rb_ref, cb_ref, ob_ref))):
            r = rev * H + hh
            nr = 2 * H
            gate = (r_ref[r], r_ref[nr + r], r_ref[2 * nr + r][:, 0:1], r_ref[3 * nr + r][:, 0:1],
                    c_ref[:, r:r + 1], c_ref[:, nr + r:nr + r + 1], c_ref[:, 2 * nr + r:2 * nr + r + 1])
            o_ref[:, vv] = _mlstm_chunk(q_ref[:, qk], k_ref[:, qk], v_ref[:, vv], gate,
                                        C_ref.at[r], n_ref.at[r], m_ref.at[r], bool(rev), L)


def mlstm_scan(proj, rows, cols):
    S = proj.shape[0]
    H, L = MLSTM_HEADS, MLSTM_CHUNK
    NC = S // L
    R = rows.shape[0]
    kv = (2 * H * MLSTM_DQK) // (H * MLSTM_DV)
    r3 = rows.reshape(R, 1, S)

    def specs(rev):
        def ch(c):
            return NC - 1 - c if rev else c
        return [pl.BlockSpec((L, H * MLSTM_DQK), lambda c: (ch(c), 0)),
                pl.BlockSpec((L, H * MLSTM_DQK), lambda c: (ch(c), 1)),
                pl.BlockSpec((L, H * MLSTM_DV), lambda c: (ch(c), kv)),
                pl.BlockSpec((R, 1, L), lambda c: (0, 0, ch(c))),
                pl.BlockSpec((L, LANES), lambda c: (ch(c), 0))]

    return pl.pallas_call(
        functools.partial(_mlstm_body, L=L, H=H),
        grid=(NC,),
        in_specs=specs(0) + specs(1),
        out_specs=[pl.BlockSpec((L, H * MLSTM_DV), lambda c: (c, 0)),
                   pl.BlockSpec((L, H * MLSTM_DV), lambda c: (NC - 1 - c, 0))],
        out_shape=[_sds((S, H * MLSTM_DV), F32), _sds((S, H * MLSTM_DV), F32)],
        scratch_shapes=[pltpu.VMEM((2 * H, MLSTM_DQK, MLSTM_DV), F32),
                        pltpu.VMEM((2 * H, 1, MLSTM_DQK), F32),
                        pltpu.VMEM((2 * H, 1, 1), F32)],
        compiler_params=_cp(("arbitrary",)),
        name="mlstm_scan",
    )(proj, proj, proj, r3, cols, proj, proj, proj, r3, cols)


def _mlstm_out_body(hf_ref, hb_ref, o_ref, hn_ref, y_ref):
    hs = hf_ref[...] + hb_ref[...]
    for h in range(MLSTM_HEADS):
        cols = slice(h * MLSTM_DV, (h + 1) * MLSTM_DV)
        x = hs[:, cols]
        x = x * lax.rsqrt(jnp.mean(x * x, axis=-1, keepdims=True) + RMS_EPS) * hn_ref[:, cols]
        y_ref[:, cols] = (x * jax.nn.sigmoid(o_ref[:, cols].astype(F32))).astype(y_ref.dtype)


def mlstm_out(hf, hb, proj, head_norm, tm=256):
    S, D = hf.shape
    ob = (2 * MLSTM_HEADS * MLSTM_DQK + D) // D
    return pl.pallas_call(
        _mlstm_out_body,
        grid=(S // tm,),
        in_specs=[pl.BlockSpec((tm, D), lambda i: (i, 0)),
                  pl.BlockSpec((tm, D), lambda i: (i, 0)),
                  pl.BlockSpec((tm, D), lambda i: (i, ob)),
                  pl.BlockSpec((1, D), lambda i: (0, 0))],
        out_specs=pl.BlockSpec((tm, D), lambda i: (i, 0)),
        out_shape=_sds((S, D), BF16),
        compiler_params=_cp(("parallel",)),
        name="mlstm_out",
    )(hf, hb, proj, head_norm.reshape(1, D))


def _excl_prefix(mask, U, Ls):
    W = _dot(mask.astype(BF16), U)
    totb = jnp.broadcast_to(W[:, LANES - 1:LANES], W.shape)
    offb = _dot(Ls, totb.astype(BF16))
    return offb + W - mask, W, offb


def _route_body(lg_ref, idx_ref, gate_ref, aff_s, gt_s, eq_s, need_s, *, NCH, cap):
    E = lg_ref.shape[0]
    lg = lg_ref[...]
    mx = jnp.max(lg, axis=0, keepdims=True)
    ex = jnp.exp(lg - mx)
    aff = ex / jnp.sum(ex, axis=0, keepdims=True)
    bits = lax.bitcast_convert_type(aff, I32)

    def count(m):
        return jnp.sum(jnp.sum(m.astype(F32), axis=1, keepdims=True), axis=2, keepdims=True)

    def search(it, T):
        cand = T | lax.shift_left(jnp.int32(1), 30 - it)
        return jnp.where(count(bits >= cand) >= cap, cand, T)

    T = lax.fori_loop(0, 31, search, jnp.zeros((E, 1, 1), I32))
    gt = bits > T
    aff_s[...] = aff
    gt_s[...] = gt.astype(F32)
    eq_s[...] = (bits == T).astype(F32)
    need_s[...] = jnp.broadcast_to(cap - count(gt), need_s.shape)

    r = lax.broadcasted_iota(I32, (LANES, LANES), 0)
    cl = lax.broadcasted_iota(I32, (LANES, LANES), 1)
    U = (r <= cl).astype(BF16)
    r2 = lax.broadcasted_iota(I32, (NCH, NCH), 0)
    c2 = lax.broadcasted_iota(I32, (NCH, NCH), 1)
    Ls = (c2 < r2).astype(BF16)
    j_row = lax.broadcasted_iota(I32, (1, cap), 1).astype(F32)
    c_iota = lax.broadcasted_iota(I32, (NCH, cap), 0).astype(F32)
    l_iota = lax.broadcasted_iota(I32, (LANES, cap), 0).astype(F32)

    def per_expert(e, carry):
        eq = eq_s[e]
        need = need_s[e][0:1, 0:1]
        rank_eq, _, _ = _excl_prefix(eq, U, Ls)
        sel = gt_s[e] + eq * (rank_eq < need).astype(F32)
        _, W, offb = _excl_prefix(sel, U, Ls)
        off_col = offb[:, 0:1]
        incl_col = off_col + W[:, LANES - 1:LANES]
        cidx = jnp.sum((incl_col <= j_row).astype(F32), axis=0, keepdims=True)
        onehot = (c_iota == cidx).astype(F32)
        local = j_row - jnp.sum(off_col * onehot, axis=0, keepdims=True)
        oh = onehot.astype(BF16)
        Wsel = _dot(W.T.astype(BF16), oh)
        lidx = jnp.sum((Wsel <= local).astype(F32), axis=0, keepdims=True)
        idx_ref[e] = (cidx * LANES + lidx).astype(I32)
        at = aff_s[e].T
        hi = at.astype(BF16)
        r1 = at - hi.astype(F32)
        mid = r1.astype(BF16)
        lo = (r1 - mid.astype(F32)).astype(BF16)
        asel = _dot(hi, oh) + _dot(mid, oh) + _dot(lo, oh)
        gate_ref[e] = jnp.sum(jnp.where(l_iota == lidx, asel, 0.0), axis=0, keepdims=True)
        return carry

    lax.fori_loop(0, E, per_expert, 0)


def route(logits_t, cap):
    E, S = logits_t.shape
    NCH = S // LANES
    full = pl.BlockSpec((E, NCH, LANES), lambda: (0, 0, 0))
    ospec = pl.BlockSpec((E, 1, cap), lambda: (0, 0, 0))
    return pl.pallas_call(
        functools.partial(_route_body, NCH=NCH, cap=cap),
        in_specs=[full],
        out_specs=[ospec, ospec],
        out_shape=[_sds((E, 1, cap), I32), _sds((E, 1, cap), F32)],
        scratch_shapes=[pltpu.VMEM((E, NCH, LANES), F32), pltpu.VMEM((E, NCH, LANES), F32),
                        pltpu.VMEM((E, NCH, LANES), F32), pltpu.VMEM((E, 8, LANES), F32)],
        compiler_params=pltpu.CompilerParams(vmem_limit_bytes=VMEM_LIMIT),
        name="route",
    )(logits_t.reshape(E, NCH, LANES))


SUBLANES = 8


def _tiled_rows(x):
    return x.reshape(x.shape[0] // SUBLANES, SUBLANES, x.shape[1])


def _issue_row_copies(idx_ref, tm, make_copy):
    def body(g, carry):
        for k in range(SUBLANES):
            row = idx_ref[0, 0, g * SUBLANES + k]
            make_copy(g, k, lax.shift_right_logical(row, 3), row & (SUBLANES - 1)).start()
        return carry
    lax.fori_loop(0, tm // SUBLANES, body, 0)


def _issue_row_gather(idx_ref, src_hbm, dst, sem, tm):
    _issue_row_copies(idx_ref, tm, lambda g, k, tile, sub: pltpu.make_async_copy(
        src_hbm.at[tile, pl.ds(sub, 1)], dst.at[g, pl.ds(k, 1)], sem))


def _moe_up_body(idx_ref, idxn_ref, hn_hbm, wg_ref, wu_ref, o_ref, buf, sem, *, tm):
    nt = pl.num_programs(1)
    step = pl.program_id(0) * nt + pl.program_id(1)
    nsteps = pl.num_programs(0) * nt
    slot = step % 2

    @pl.when(step == 0)
    def _():
        _issue_row_gather(idx_ref, hn_hbm, buf.at[0], sem.at[0], tm)

    def wait_rows(s):
        pltpu.make_async_copy(hn_hbm.at[pl.ds(0, tm // SUBLANES)], buf.at[s], sem.at[s]).wait()

    wait_rows(slot)
    half = buf.shape[-1]
    kc, rc = 2 * half // MOE_K_CHUNKS, tm // MOE_K_CHUNKS
    g = u = None
    for c in range(MOE_K_CHUNKS):
        pc = (c * kc) % half
        xs = _unpack_bf16_pair(buf[slot, :, :, pc:pc + kc].reshape(tm, kc), high=c * kc >= half)
        gc = _dot(xs, wg_ref[c * kc:(c + 1) * kc, :])
        uc = _dot(xs, wu_ref[c * kc:(c + 1) * kc, :])
        for r in range(c * rc, (c + 1) * rc):
            row = idxn_ref[0, 0, r]
            pltpu.make_async_copy(
                hn_hbm.at[lax.shift_right_logical(row, 3), pl.ds(row & (SUBLANES - 1), 1)],
                buf.at[1 - slot, r // SUBLANES, pl.ds(r % SUBLANES, 1)], sem.at[1 - slot]).start()
        g, u = (gc, uc) if g is None else (g + gc, u + uc)
    o_ref[...] = (g * jax.nn.sigmoid(g) * u).astype(o_ref.dtype)

    @pl.when(step == nsteps - 1)
    def _():
        wait_rows(1 - slot)


def moe_up(hn, idx, wg, wu, layer, tm):
    E, _, cap = idx.shape
    D, F = wg.shape[2], wg.shape[3]
    assert hn.shape[1] * 2 == D and hn.dtype == jnp.uint32
    nt = cap // tm
    nsteps = E * nt

    def nxt(e, t):
        s = jnp.minimum(e * nt + t + 1, nsteps - 1)
        return (s // nt, 0, s % nt)

    wspec = pl.BlockSpec((None, None, D, F), lambda e, t: (layer, e, 0, 0))
    return pl.pallas_call(
        functools.partial(_moe_up_body, tm=tm),
        grid=(E, nt),
        in_specs=[pl.BlockSpec((1, 1, tm), lambda e, t: (e, 0, t), memory_space=pltpu.SMEM),
                  pl.BlockSpec((1, 1, tm), nxt, memory_space=pltpu.SMEM),
                  pl.BlockSpec(memory_space=pl.ANY),
                  wspec, wspec],
        out_specs=pl.BlockSpec((tm, F), lambda e, t: (e * nt + t, 0)),
        out_shape=_sds((E * cap, F), BF16),
        scratch_shapes=[pltpu.VMEM((2, tm // SUBLANES, SUBLANES, D // 2), jnp.uint32),
                        pltpu.SemaphoreType.DMA((2,))],
        compiler_params=_cp(("arbitrary", "arbitrary")),
        name="moe_up",
    )(idx, idx, _tiled_rows(hn), wg, wu)


DOWN_SLOTS = 3


def _moe_down_body(idx_ref, idxn_ref, gate_ref, h_ref, wd_ref, x_hbm, xo_hbm, xbuf, gsem, ssem, *, tm):
    del x_hbm
    nt = pl.num_programs(1)
    t = pl.program_id(1)
    step = pl.program_id(0) * nt + t
    nsteps = pl.num_programs(0) * nt
    cur = step % DOWN_SLOTS
    nxt = (step + 1) % DOWN_SLOTS
    prv = (step + 2) % DOWN_SLOTS

    def wait_scatter(slot):
        pltpu.make_async_copy(xbuf.at[slot], xo_hbm.at[pl.ds(0, tm // SUBLANES)], ssem.at[slot]).wait()

    @pl.when((step >= 2) & (t != 1))
    def _():
        wait_scatter(nxt)

    @pl.when((t == 0) & (step >= 1))
    def _():
        wait_scatter(prv)

    @pl.when(t == 0)
    def _():
        _issue_row_gather(idx_ref, xo_hbm, xbuf.at[cur], gsem.at[cur], tm)

    @pl.when(t + 1 < nt)
    def _():
        _issue_row_gather(idxn_ref, xo_hbm, xbuf.at[nxt], gsem.at[nxt], tm)

    y = _dot(h_ref[...], wd_ref[...])
    t_idx = lax.broadcasted_iota(I32, (tm, tm), 0)
    s_idx = lax.broadcasted_iota(I32, (tm, tm), 1)
    gate_col = _row_to_col(gate_ref[0], t_idx == s_idx)
    pltpu.make_async_copy(xo_hbm.at[pl.ds(0, tm // SUBLANES)], xbuf.at[cur], gsem.at[cur]).wait()
    upd = xbuf[cur].reshape(y.shape) + y * gate_col
    xbuf[cur] = upd.reshape(xbuf.shape[1:])
    _issue_row_copies(idx_ref, tm, lambda g, k, tile, sub: pltpu.make_async_copy(
        xbuf.at[cur, g, pl.ds(k, 1)], xo_hbm.at[tile, pl.ds(sub, 1)], ssem.at[cur]))

    @pl.when(step == nsteps - 1)
    def _():
        wait_scatter(prv)
        wait_scatter(cur)


def moe_down(x, hmid, idx, gate, wd, layer, tm):
    E, _, cap = idx.shape
    S, D = x.shape
    F = wd.shape[2]
    nt = cap // tm
    assert nt >= 2, "the scatter wait schedule needs at least two tiles per expert"
    nsteps = E * nt

    def nxt(e, t):
        s = jnp.minimum(e * nt + t + 1, nsteps - 1)
        return (s // nt, 0, s % nt)

    return pl.pallas_call(
        functools.partial(_moe_down_body, tm=tm),
        grid=(E, nt),
        in_specs=[pl.BlockSpec((1, 1, tm), lambda e, t: (e, 0, t), memory_space=pltpu.SMEM),
                  pl.BlockSpec((1, 1, tm), nxt, memory_space=pltpu.SMEM),
                  pl.BlockSpec((None, 1, tm), lambda e, t: (e, 0, t)),
                  pl.BlockSpec((tm, F), lambda e, t: (e * nt + t, 0)),
                  pl.BlockSpec((None, None, F, D), lambda e, t: (layer, e, 0, 0)),
                  pl.BlockSpec(memory_space=pl.ANY)],
        out_specs=pl.BlockSpec(memory_space=pl.ANY),
        out_shape=_sds((S // SUBLANES, SUBLANES, D), F32),
        input_output_aliases={5: 0},
        scratch_shapes=[pltpu.VMEM((DOWN_SLOTS, tm // SUBLANES, SUBLANES, D), F32),
                        pltpu.SemaphoreType.DMA((DOWN_SLOTS,)),
                        pltpu.SemaphoreType.DMA((DOWN_SLOTS,))],
        compiler_params=_cp(("arbitrary", "arbitrary")),
        name="moe_down",
    )(idx, idx, gate, hmid, wd, _tiled_rows(x)).reshape(S, D)


def expert_choice_ffn(x, norm_g, w_router, w_gate, w_up, w_down, layer):
    S = x.shape[0]
    E = w_router.shape[1]
    cap = CAPACITY_FACTOR * S // E
    hn, logits_t = rmsnorm_proj(x, norm_g, w_router, jnp.zeros((E,), F32), None, pack=True)
    idx, gate = route(logits_t, cap)
    tm = min(MOE_TILE, cap // 2)
    hmid = moe_up(hn, idx, w_gate, w_up, layer, tm)
    return moe_down(x, hmid, idx, gate, w_down, layer, tm)


def conformer_mixer(x, norm_g, w_in, b_in, w_dw, b_dw, ln_g, ln_b, w_out, b_out, j):
    h = rmsnorm(x, norm_g, BF16)
    u = matmul_glu(h, w_in, j, b_in)
    u = dwconv_ln_silu(u, w_dw, b_dw, ln_g, ln_b)
    return matmul_bias_residual(u, w_out, j, b_out, x)


def attention_mixer(x, norm_g, positions, w_qkv, w_out, j):
    h = rmsnorm(x, norm_g, BF16)
    tabs = rope_tables(positions)
    outs, lses = [], []
    for g, (window, dil) in enumerate(DIL_PATTERNS):
        assert window // (2 * dil) == ATTN_HALF
        o, lse = band_attention(matmul_qkv_rope(h, w_qkv, tabs, g, dil), dil)
        outs.append(o)
        lses.append(lse)
    o = attn_combine(outs, lses)
    return matmul_bias_residual(o, w_out, j, jnp.zeros((w_out.shape[2],), F32), x)


def mlstm_mixer(x, norm_g, w_in, b_gates, head_norm, w_out, j):
    D = x.shape[1]
    n_main = 2 * MLSTM_HEADS * MLSTM_DQK + 2 * D
    h, gates = rmsnorm_proj(x, norm_g, w_in[:, n_main:], b_gates, BF16)
    proj = matmul_plain(h, w_in[:, :n_main].astype(BF16), BF16)
    hf, hb = mlstm_scan(proj, *mlstm_gates(gates))
    y = mlstm_out(hf, hb, proj, head_norm)
    return matmul_bias_residual(y, w_out, j, jnp.zeros((D,), F32), x)


def kernel(x, positions, mix_norm, ffn_norm, conv_w_in, conv_b_in, conv_w_dw, conv_b_dw, conv_ln_g,
           conv_ln_b, conv_w_out, conv_b_out, attn_w_qkv, attn_w_out, mlstm_w_in, mlstm_b_gates,
           mlstm_head_norm, mlstm_w_out, router_w, moe_w_gate, moe_w_up, moe_w_down, final_norm):
    assert x.shape[0] == 1, "single-sequence batch"
    xs = x[0]
    pos = positions[0]
    depth = mix_norm.shape[0]
    conv_w_in, conv_w_out = conv_w_in.astype(BF16), conv_w_out.astype(BF16)
    attn_w_out, mlstm_w_out = attn_w_out.astype(BF16), mlstm_w_out.astype(BF16)
    moe_w_gate, moe_w_up, moe_w_down = (w.astype(BF16) for w in (moe_w_gate, moe_w_up, moe_w_down))
    for i in range(depth):
        kind, j = i % 3, i // 3
        if kind == 0:
            xs = conformer_mixer(xs, mix_norm[i], conv_w_in, conv_b_in[j], conv_w_dw[j], conv_b_dw[j],
                                 conv_ln_g[j], conv_ln_b[j], conv_w_out, conv_b_out[j], j)
        elif kind == 1:
            xs = attention_mixer(xs, mix_norm[i], pos, attn_w_qkv[j].astype(BF16), attn_w_out, j)
        else:
            xs = mlstm_mixer(xs, mix_norm[i], mlstm_w_in[j], mlstm_b_gates[j], mlstm_head_norm[j],
                             mlstm_w_out, j)
        xs = expert_choice_ffn(xs, ffn_norm[i], router_w[i], moe_w_gate, moe_w_up, moe_w_down, i)
    return rmsnorm(xs, final_norm, F32)[None]
```
